```python
import jax, jax.numpy as jnp
from jax import lax
import numpy as np

D_MODEL = 1024
BATCH = 8
SEQ = 2048
DEPTH = 4
DEC_BATCH = 128
DEC_SEQ = 8
PAST_LEN = 8192
PAGE_SIZE = 128

N_MIXERS = 2
HEAD_DIM = 64
N_HEADS = D_MODEL // HEAD_DIM
N_KV_HEADS = 4
GROUP = N_HEADS // N_KV_HEADS
D_ATT = N_HEADS * HEAD_DIM
D_KV = N_KV_HEADS * HEAD_DIM
WINDOW = 128
SB_BLOCK = 128
SB_BIAS_INIT = -5.0
ROPE_THETA = 10000.0
D_FF = 2816
CONV_W = 3
RMS_EPS = 1e-6
ATTN_SCALE = HEAD_DIM ** -0.5
N_A_LAYERS = (DEPTH + 1) // 2
N_B_LAYERS = DEPTH // 2

kernel_name = 'swa_sink_stickbreak_convffn_decode_step'


def rmsnorm(x, g):
    xf = x.astype(jnp.float32)
    r = lax.rsqrt(jnp.mean(xf * xf, axis=-1, keepdims=True) + RMS_EPS)
    return (xf * r * g.astype(jnp.float32)).astype(x.dtype)


def rope(x, pos):
    inv_freq = ROPE_THETA ** (-jnp.arange(0, HEAD_DIM, 2, dtype=jnp.float32) / HEAD_DIM)
    ang = pos.astype(jnp.float32)[:, None] * inv_freq[None, :]
    cos = jnp.cos(ang)[None, :, None, :]
    sin = jnp.sin(ang)[None, :, None, :]
    xf = x.astype(jnp.float32)
    x1, x2 = xf[..., :HEAD_DIM // 2], xf[..., HEAD_DIM // 2:]
    return jnp.concatenate([x1 * cos - x2 * sin, x2 * cos + x1 * sin], axis=-1).astype(x.dtype)


def split_qkv(h, w_qkv):
    n, t = h.shape[0], h.shape[1]
    qkv = h @ w_qkv
    q = qkv[..., :D_ATT].reshape(n, t, N_HEADS, HEAD_DIM)
    k = qkv[..., D_ATT:D_ATT + D_KV].reshape(n, t, N_KV_HEADS, HEAD_DIM)
    v = qkv[..., D_ATT + D_KV:].reshape(n, t, N_KV_HEADS, HEAD_DIM)
    return q, k, v


def swa_attend(q, k, v, q_pos, k_pos, sink):
    n, tq = q.shape[0], q.shape[1]
    qg = q.reshape(n, tq, N_KV_HEADS, GROUP, HEAD_DIM)
    z = jnp.einsum('nqkgd,nskd->nkgqs', qg, k).astype(jnp.float32) * ATTN_SCALE
    dist = q_pos[:, :, None] - k_pos[:, None, :]
    mask = (dist >= 0) & (dist <= WINDOW) & (k_pos[:, None, :] >= 0)
    z = jnp.where(mask[:, None, None], z, -jnp.inf)
    sk = sink.astype(jnp.float32).reshape(1, N_KV_HEADS, GROUP, 1, 1)
    m = jnp.maximum(jnp.max(z, axis=-1, keepdims=True), sk)
    p = jnp.exp(z - m)
    p = p / (jnp.sum(p, axis=-1, keepdims=True) + jnp.exp(sk - m))
    o = jnp.einsum('nkgqs,nskd->nqkgd', p.astype(v.dtype), v)
    return o.reshape(n, tq, D_ATT)


def sb_attend(q, k, v, q_pos, k_pos, bias):
    n, tq = q.shape[0], q.shape[1]
    qg = q.reshape(n, tq, N_KV_HEADS, GROUP, HEAD_DIM)
    z = jnp.einsum('nqkgd,nskd->nkgqs', qg, k).astype(jnp.float32) * ATTN_SCALE
    z = z + bias.astype(jnp.float32).reshape(1, N_KV_HEADS, GROUP, 1, 1)
    mask = k_pos[None, :] < q_pos[:, None]
    log_rest = jnp.where(mask, jax.nn.log_sigmoid(-z), 0.0)
    suffix = lax.cumsum(log_rest, axis=z.ndim - 1, reverse=True) - log_rest
    a = jnp.where(mask, jnp.exp(jax.nn.log_sigmoid(z) + suffix), 0.0)
    o = jnp.einsum('nkgqs,nskd->nqkgd', a.astype(v.dtype), v)
    return o.reshape(n, tq, D_ATT)


def swa_prompt(h, w_qkv, sink):
    b, t = h.shape[0], h.shape[1]
    q, k, v = split_qkv(h, w_qkv)
    pos = jnp.arange(t)
    q, k = rope(q, pos), rope(k, pos)
    nb = t // WINDOW
    qb = q.reshape(b * nb, WINDOW, N_HEADS, HEAD_DIM)

    def band(xx):
        xb = xx.reshape(b, nb, WINDOW, N_KV_HEADS, HEAD_DIM)
        prev = jnp.concatenate([jnp.zeros_like(xb[:, :1]), xb[:, :-1]], axis=1)
        return jnp.concatenate([prev, xb], axis=2).reshape(b * nb, 2 * WINDOW, N_KV_HEADS, HEAD_DIM)

    q_pos = pos.reshape(nb, WINDOW)
    k_pos = jnp.concatenate([q_pos - WINDOW, q_pos], axis=1)
    o = swa_attend(qb, band(k), band(v), jnp.tile(q_pos, (b, 1)), jnp.tile(k_pos, (b, 1)), sink)
    return o.reshape(b, t, D_ATT), k[:, t - WINDOW:], v[:, t - WINDOW:]


def swa_sample(h, w_qkv, sink, buf_k, buf_v):
    t = h.shape[1]
    w_buf = buf_k.shape[1]
    q, k, v = split_qkv(h, w_qkv)
    pos = PAST_LEN + jnp.arange(t)
    q, k = rope(q, pos), rope(k, pos)
    k_all = jnp.concatenate([buf_k.astype(k.dtype), k], axis=1)
    v_all = jnp.concatenate([buf_v.astype(v.dtype), v], axis=1)
    k_pos = jnp.concatenate([PAST_LEN - w_buf + jnp.arange(w_buf), pos])
    o = swa_attend(q, k_all, v_all, pos[None], k_pos[None], sink)
    return o, k_all[:, -w_buf:], v_all[:, -w_buf:]


def sb_prompt(h, w_qkv, bias):
    b, t = h.shape[0], h.shape[1]
    q, k, v = split_qkv(h, w_qkv)
    nb = t // SB_BLOCK
    qb = q.reshape(b, nb, SB_BLOCK, N_HEADS, HEAD_DIM).transpose(1, 0, 2, 3, 4)
    q_pos = jnp.arange(t).reshape(nb, SB_BLOCK)
    k_pos = jnp.arange(t)
    o = lax.map(lambda a: sb_attend(a[0], k, v, a[1], k_pos, bias), (qb, q_pos))
    return o.transpose(1, 0, 2, 3).reshape(b, t, D_ATT), k, v


def sb_sample(h, w_qkv, bias, pool_k, pool_v, page_table):
    n, t = h.shape[0], h.shape[1]
    past_len = page_table.shape[1] * pool_k.shape[1]
    q, k, v = split_qkv(h, w_qkv)
    k_past = pool_k[page_table].reshape(n, past_len, N_KV_HEADS, HEAD_DIM).astype(k.dtype)
    v_past = pool_v[page_table].reshape(n, past_len, N_KV_HEADS, HEAD_DIM).astype(v.dtype)
    k_all = jnp.concatenate([k_past, k], axis=1)
    v_all = jnp.concatenate([v_past, v], axis=1)
    q_pos = past_len + jnp.arange(t)
    k_pos = jnp.arange(past_len + t)
    o = sb_attend(q, k_all, v_all, q_pos, k_pos, bias)
    return o, k, v


def conv_ffn(h, w_in, cw, cb, w_down, prev):
    t = h.shape[1]
    gu = h @ w_in
    gate, up = gu[..., :D_FF], gu[..., D_FF:]
    xp = jnp.concatenate([prev.astype(gate.dtype), gate], axis=1)
    gc = cb
    for tap in range(CONV_W):
        gc = gc + cw[tap] * xp[:, tap:tap + t]
    y = (jax.nn.silu(gc) * up) @ w_down
    return y, xp[:, t:]


def setup_inputs(seed: int = 0) -> dict:
    key = jax.random.key(seed)
    ks = jax.random.split(key, 24)
    n_pages = PAST_LEN // PAGE_SIZE
    n_pool = (DEC_BATCH * n_pages * 5) // 4
    w_buf = min(WINDOW, PAST_LEN)
    f32 = jnp.float32
    nrm = lambda k, shape, s: jax.random.normal(k, shape, f32) * s
    page_table = jax.random.permutation(ks[7], n_pool)[:DEC_BATCH * n_pages].reshape(DEC_BATCH, n_pages).astype(jnp.int32)
    return {
        'x_prompt': nrm(ks[0], (BATCH, SEQ, D_MODEL), 1.0),
        'x_sample': nrm(ks[1], (DEC_BATCH, DEC_SEQ, D_MODEL), 1.0),
        'cache_swa_k': nrm(ks[2], (N_A_LAYERS, DEC_BATCH, w_buf, N_KV_HEADS, HEAD_DIM), 1.0),
        'cache_swa_v': nrm(ks[3], (N_A_LAYERS, DEC_BATCH, w_buf, N_KV_HEADS, HEAD_DIM), 1.0),
        'cache_sb_k': nrm(ks[4], (N_B_LAYERS, n_pool, PAGE_SIZE, N_KV_HEADS, HEAD_DIM), 1.0),
        'cache_sb_v': nrm(ks[5], (N_B_LAYERS, n_pool, PAGE_SIZE, N_KV_HEADS, HEAD_DIM), 1.0),
        'state_conv': nrm(ks[6], (DEPTH, DEC_BATCH, CONV_W - 1, D_FF), 0.5),
        'page_table': page_table,
        'norm_mix': 1.0 + nrm(ks[8], (DEPTH, D_MODEL), 0.02),
        'norm_ffn': 1.0 + nrm(ks[9], (DEPTH, D_MODEL), 0.02),
        'norm_final': 1.0 + nrm(ks[10], (D_MODEL,), 0.02),
        'w_qkv_a': nrm(ks[11], (N_A_LAYERS, D_MODEL, D_ATT + 2 * D_KV), D_MODEL ** -0.5),
        'sinks_a': nrm(ks[12], (N_A_LAYERS, N_HEADS), 0.5),
        'w_o_a': nrm(ks[13], (N_A_LAYERS, D_ATT, D_MODEL), D_ATT ** -0.5),
        'w_qkv_b': nrm(ks[14], (N_B_LAYERS, D_MODEL, D_ATT + 2 * D_KV), D_MODEL ** -0.5),
        'bias_b': SB_BIAS_INIT + nrm(ks[20], (N_B_LAYERS, N_HEADS), 0.5),
        'w_o_b': nrm(ks[15], (N_B_LAYERS, D_ATT, D_MODEL), D_ATT ** -0.5),
        'w_in_ffn': nrm(ks[16], (DEPTH, D_MODEL, 2 * D_FF), D_MODEL ** -0.5),
        'conv_w': nrm(ks[17], (DEPTH, CONV_W, D_FF), CONV_W ** -0.5),
        'conv_b': nrm(ks[18], (DEPTH, D_FF), 0.02),
        'w_down': nrm(ks[19], (DEPTH, D_FF, D_MODEL), D_FF ** -0.5),
    }


def reference(x_prompt, x_sample, cache_swa_k, cache_swa_v, cache_sb_k, cache_sb_v, state_conv, page_table,
              norm_mix, norm_ffn, norm_final, w_qkv_a, sinks_a, w_o_a, w_qkv_b, bias_b, w_o_b,
              w_in_ffn, conv_w, conv_b, w_down):
    xp, xs = x_prompt, x_sample
    swa_kp, swa_vp, swa_ks, swa_vs = [], [], [], []
    sb_kp, sb_vp, sb_ks, sb_vs = [], [], [], []
    conv_p, conv_s = [], []
    for i in range(DEPTH):
        j = i // N_MIXERS
        hp = rmsnorm(xp, norm_mix[i])
        hs = rmsnorm(xs, norm_mix[i])
        if i % N_MIXERS == 0:
            op, kp, vp = swa_prompt(hp, w_qkv_a[j], sinks_a[j])
            os_, k_s, v_s = swa_sample(hs, w_qkv_a[j], sinks_a[j], cache_swa_k[j], cache_swa_v[j])
            w_o = w_o_a[j]
            swa_kp.append(kp); swa_vp.append(vp); swa_ks.append(k_s); swa_vs.append(v_s)
        else:
            op, kp, vp = sb_prompt(hp, w_qkv_b[j], bias_b[j])
            os_, k_s, v_s = sb_sample(hs, w_qkv_b[j], bias_b[j], cache_sb_k[j], cache_sb_v[j], page_table)
            w_o = w_o_b[j]
            sb_kp.append(kp); sb_vp.append(vp); sb_ks.append(k_s); sb_vs.append(v_s)
        xp = xp + op @ w_o
        xs = xs + os_ @ w_o
        zeros_prev = jnp.zeros((xp.shape[0], CONV_W - 1, D_FF), xp.dtype)
        fp, cp = conv_ffn(rmsnorm(xp, norm_ffn[i]), w_in_ffn[i], conv_w[i], conv_b[i], w_down[i], zeros_prev)
        fs, cs = conv_ffn(rmsnorm(xs, norm_ffn[i]), w_in_ffn[i], conv_w[i], conv_b[i], w_down[i], state_conv[i])
        xp = xp + fp
        xs = xs + fs
        conv_p.append(cp); conv_s.append(cs)
    y_prompt = rmsnorm(xp, norm_final)
    y_sample = rmsnorm(xs, norm_final)
    return (y_prompt, y_sample,
            jnp.stack(swa_kp), jnp.stack(swa_vp), jnp.stack(sb_kp), jnp.stack(sb_vp), jnp.stack(conv_p),
            jnp.stack(swa_ks), jnp.stack(swa_vs), jnp.stack(sb_ks), jnp.stack(sb_vs), jnp.stack(conv_s))
```

```python
import functools

import jax
import jax.numpy as jnp
from jax import lax
from jax.experimental import pallas as pl
from jax.experimental.pallas import tpu as pltpu

F32 = jnp.float32
BF16 = jnp.bfloat16

D_MODEL = 1024
BATCH = 8
SEQ = 2048
DEPTH = 4
DEC_BATCH = 128
DEC_SEQ = 8
PAST_LEN = 8192
PAGE_SIZE = 128
HEAD_DIM = 64
N_HEADS = 16
N_KV_HEADS = 4
GROUP = N_HEADS // N_KV_HEADS
D_KV = N_KV_HEADS * HEAD_DIM
D_QKV = D_MODEL + 2 * D_KV
WINDOW = 128
ROPE_THETA = 10000.0
D_FF = 2816
CONV_W = 3
RMS_EPS = 1e-6
ATTN_SCALE = HEAD_DIM ** -0.5
N_PAGES = PAST_LEN // PAGE_SIZE

LANES = 128
SUBLANES = 8
FF_CHUNK = 256
N_FF_CHUNKS = D_FF // FF_CHUNK
TOKEN_TILE = 512
PAGES_PER_STEP = 16
VMEM_LIMIT = 56 * 1024 * 1024

assert D_FF % FF_CHUNK == 0 and SEQ % TOKEN_TILE == 0 and N_PAGES % PAGES_PER_STEP == 0
assert DEC_SEQ == SUBLANES and WINDOW == LANES and PAGE_SIZE == LANES and 2 * HEAD_DIM == LANES


def _params(n_axes):
    return pltpu.CompilerParams(dimension_semantics=("arbitrary",) * n_axes,
                                vmem_limit_bytes=VMEM_LIMIT)


def _resident(shape, index_map):
    return pl.BlockSpec(shape, index_map, pipeline_mode=pl.Buffered(1))


def _rmsnorm(x, g):
    r = lax.rsqrt(jnp.mean(x * x, axis=-1, keepdims=True) + RMS_EPS)
    return x * r * g


def _dot(a, b):
    return jnp.dot(a, b, preferred_element_type=F32)


def _dot_nt(a, b):
    return lax.dot_general(a, b, (((1,), (1,)), ((), ())), preferred_element_type=F32)


def _rope_tile(x, cosf, sinf):
    lane = lax.broadcasted_iota(jnp.int32, x.shape, 1)
    first = (lane % HEAD_DIM) < (HEAD_DIM // 2)
    other = jnp.where(first, pltpu.roll(x, LANES - HEAD_DIM // 2, axis=1),
                      pltpu.roll(x, HEAD_DIM // 2, axis=1))
    return x * cosf + other * sinf


def _qkv_body(*refs, rope):
    if rope:
        x_ref, g_ref, w_ref, cos_ref, sin_ref, q_ref, k_ref, v_ref = refs
    else:
        x_ref, g_ref, w_ref, q_ref, k_ref, v_ref = refs
    h = _rmsnorm(x_ref[...], g_ref[...]).astype(BF16)
    qkv = _dot(h, w_ref[...])
    if rope:
        cosf, sinf = cos_ref[...], sin_ref[...]
        for j in range(D_MODEL // LANES):
            sl = slice(j * LANES, (j + 1) * LANES)
            q_ref[:, sl] = _rope_tile(qkv[:, sl], cosf, sinf) * ATTN_SCALE
        for j in range(D_KV // LANES):
            sl = slice(j * LANES, (j + 1) * LANES)
            k_ref[:, sl] = _rope_tile(qkv[:, D_MODEL + j * LANES:D_MODEL + (j + 1) * LANES], cosf, sinf)
    else:
        q_ref[...] = qkv[:, :D_MODEL] * ATTN_SCALE
        k_ref[...] = qkv[:, D_MODEL:D_MODEL + D_KV]
    v_ref[...] = qkv[:, D_MODEL + D_KV:]


def _qkv_call(x, g, w, cos_sin, table_tiles, name):
    t = x.shape[0]
    tm = TOKEN_TILE
    rope = cos_sin is not None
    in_specs = [pl.BlockSpec((tm, D_MODEL), lambda i: (i, 0)),
                pl.BlockSpec((1, D_MODEL), lambda i: (0, 0)),
                pl.BlockSpec((D_MODEL, D_QKV), lambda i: (0, 0))]
    args = [x, g.reshape(1, D_MODEL), w]
    if rope:
        in_specs += [pl.BlockSpec((tm, LANES), lambda i: (i % table_tiles, 0))] * 2
        args += list(cos_sin)
    return pl.pallas_call(
        functools.partial(_qkv_body, rope=rope),
        grid=(t // tm,),
        in_specs=in_specs,
        out_specs=[pl.BlockSpec((tm, D_MODEL), lambda i: (i, 0)),
                   pl.BlockSpec((tm, D_KV), lambda i: (i, 0)),
                   pl.BlockSpec((tm, D_KV), lambda i: (i, 0))],
        out_shape=[jax.ShapeDtypeStruct((t, D_MODEL), F32),
                   jax.ShapeDtypeStruct((t, D_KV), F32),
                   jax.ShapeDtypeStruct((t, D_KV), F32)],
        compiler_params=_params(1),
        name=name,
    )(*args)


def _rope_tables(pos):
    inv_freq = ROPE_THETA ** (-jnp.arange(0, HEAD_DIM, 2, dtype=F32) / HEAD_DIM)
    ang = pos.astype(F32)[:, None] * inv_freq[None, :]
    cos, sin = jnp.cos(ang), jnp.sin(ang)
    cosf = jnp.concatenate([cos, cos, cos, cos], axis=1)
    sinf = jnp.concatenate([-sin, sin, -sin, sin], axis=1)
    return cosf, sinf


def _half_mask(shape, half):
    lane = lax.broadcasted_iota(jnp.int32, shape, 1)
    return (lane >= HEAD_DIM) if half else (lane < HEAD_DIM)


def _q_for_kv_tile(q_tile, src_half, dst_half):
    if src_half != dst_half:
        q_tile = pltpu.roll(q_tile, HEAD_DIM, axis=1)
    return jnp.where(_half_mask(q_tile.shape, dst_half), q_tile, 0.0)


def _merge_heads(even, odd, even_half, odd_half):
    if even_half != 0:
        even = pltpu.roll(even, HEAD_DIM, axis=1)
    if odd_half != 1:
        odd = pltpu.roll(odd, HEAD_DIM, axis=1)
    return jnp.where(_half_mask(even.shape, 0), even, odd)


def _block_diag_q(q):
    zeros = jnp.zeros((SUBLANES, LANES), F32)
    rows = []
    for kv in range(N_KV_HEADS):
        for g in range(GROUP):
            h = kv * GROUP + g
            piece = _q_for_kv_tile(q[:, (h // 2) * LANES:(h // 2 + 1) * LANES], h % 2, kv % 2)
            rows.append(jnp.concatenate([piece, zeros] if kv // 2 == 0 else [zeros, piece], axis=1))
    return jnp.concatenate(rows, axis=0)


def _unpack_block_diag(acc):
    tiles = []
    for pair in range(N_HEADS // 2):
        parts = []
        for h in (2 * pair, 2 * pair + 1):
            kv, g = h // GROUP, h % GROUP
            r0 = (kv * GROUP + g) * SUBLANES
            parts.append((acc[r0:r0 + SUBLANES, (kv // 2) * LANES:(kv // 2 + 1) * LANES], kv % 2))
        tiles.append(_merge_heads(parts[0][0], parts[1][0], parts[0][1], parts[1][1]))
    return jnp.concatenate(tiles, axis=1)


def _per_row_heads(scalar_ref, rows_per_head, width):
    return jnp.concatenate([jnp.full((rows_per_head, width), scalar_ref[h], F32) for h in range(N_HEADS)], axis=0)


def _swa_prompt_body(sink_ref, q_ref, kp_ref, kc_ref, vp_ref, vc_ref, o_ref):
    j = pl.program_id(1)
    rows = GROUP * WINDOW
    a = lax.broadcasted_iota(jnp.int32, (rows, 2 * WINDOW), 0) % WINDOW
    c = lax.broadcasted_iota(jnp.int32, (rows, 2 * WINDOW), 1)
    mask = (c >= a) & (c <= a + WINDOW) & ((j > 0) | (c >= WINDOW))
    out_parts = []
    for kv in range(N_KV_HEADS):
        lt = slice((kv // 2) * LANES, (kv // 2 + 1) * LANES)
        kb = jnp.concatenate([kp_ref[:, lt], kc_ref[:, lt]], axis=0).astype(BF16)
        vb = jnp.concatenate([vp_ref[:, lt], vc_ref[:, lt]], axis=0).astype(BF16)
        qs, sinks = [], []
        for g in range(GROUP):
            h = kv * GROUP + g
            qs.append(_q_for_kv_tile(q_ref[:, (h // 2) * LANES:(h // 2 + 1) * LANES], h % 2, kv % 2))
            sinks.append(jnp.full((WINDOW, 1), sink_ref[h], F32))
        q4 = jnp.concatenate(qs, axis=0).astype(BF16)
        sink = jnp.concatenate(sinks, axis=0)
        z = jnp.where(mask, _dot_nt(q4, kb), -jnp.inf)
        m = jnp.maximum(jnp.max(z, axis=-1, keepdims=True), sink)
        p = jnp.exp(z - m)
        denom = jnp.sum(p, axis=-1, keepdims=True) + jnp.exp(sink - m)
        o4 = _dot(p.astype(BF16), vb) / denom
        out_parts += [(o4[g * WINDOW:(g + 1) * WINDOW], kv % 2) for g in range(GROUP)]
    tiles = [_merge_heads(out_parts[2 * t][0], out_parts[2 * t + 1][0], out_parts[2 * t][1], out_parts[2 * t + 1][1])
             for t in range(N_HEADS // 2)]
    o_ref[...] = jnp.concatenate(tiles, axis=1).astype(o_ref.dtype)


def _swa_prompt_call(sink, q, k, v):
    nb = SEQ // WINDOW
    cur = lambda b, j: (b * nb + j, 0)
    prev = lambda b, j: (b * nb + jnp.maximum(j - 1, 0), 0)
    kv_spec = lambda f: pl.BlockSpec((WINDOW, D_KV), f)
    return pl.pallas_call(
        _swa_prompt_body,
        grid=(BATCH, nb),
        in_specs=[pl.BlockSpec(memory_space=pltpu.SMEM),
                  pl.BlockSpec((WINDOW, D_MODEL), cur),
                  kv_spec(prev), kv_spec(cur), kv_spec(prev), kv_spec(cur)],
        out_specs=pl.BlockSpec((WINDOW, D_MODEL), cur),
        out_shape=jax.ShapeDtypeStruct((BATCH * SEQ, D_MODEL), BF16),
        compiler_params=_params(2),
        name="swa_prompt",
    )(sink, q, k, k, v, v)


def _swa_sample_body(sink_ref, q_ref, kn_ref, vn_ref, kc_ref, vc_ref, o_ref, kw_ref, vw_ref, pad_scr):
    rows = N_HEADS * DEC_SEQ
    t = lax.broadcasted_iota(jnp.int32, (rows, WINDOW), 0) % DEC_SEQ
    c = lax.broadcasted_iota(jnp.int32, (rows, WINDOW), 1)
    kc, vc, kn, vn = kc_ref[...], vc_ref[...], kn_ref[...], vn_ref[...]
    keep = WINDOW - DEC_SEQ
    kw_ref[0:keep, :] = kc[DEC_SEQ:, :]
    kw_ref[keep:, :] = kn
    vw_ref[0:keep, :] = vc[DEC_SEQ:, :]
    vw_ref[keep:, :] = vn

    def padded(x):
        pad_scr[...] = jnp.zeros(pad_scr.shape, F32)
        pad_scr[0:DEC_SEQ, :] = x
        return pad_scr[...].astype(BF16)

    qbd = _block_diag_q(q_ref[...]).astype(BF16)
    sink = _per_row_heads(sink_ref, DEC_SEQ, 1)
    zc = jnp.where(c >= t, _dot_nt(qbd, kc.astype(BF16)), -jnp.inf)
    zn = jnp.where(c <= t, _dot_nt(qbd, padded(kn)), -jnp.inf)
    m = jnp.maximum(jnp.maximum(jnp.max(zc, axis=-1, keepdims=True), jnp.max(zn, axis=-1, keepdims=True)), sink)
    pc = jnp.exp(zc - m)
    pn = jnp.exp(zn - m)
    denom = jnp.sum(pc, axis=-1, keepdims=True) + jnp.sum(pn, axis=-1, keepdims=True) + jnp.exp(sink - m)
    acc = (_dot(pc.astype(BF16), vc.astype(BF16)) + _dot(pn.astype(BF16), padded(vn))) / denom
    o_ref[...] = _unpack_block_diag(acc)


def _swa_sample_call(sink, q, k_new, v_new, k_cache, v_cache):
    seq_rows = lambda s: (s, 0)
    cache = lambda s: (s, 0, 0)
    cache_spec = pl.BlockSpec((None, WINDOW, D_KV), cache)
    return pl.pallas_call(
        _swa_sample_body,
        grid=(DEC_BATCH,),
        in_specs=[pl.BlockSpec(memory_space=pltpu.SMEM),
                  pl.BlockSpec((DEC_SEQ, D_MODEL), seq_rows),
                  pl.BlockSpec((DEC_SEQ, D_KV), seq_rows),
                  pl.BlockSpec((DEC_SEQ, D_KV), seq_rows),
                  cache_spec, cache_spec],
        out_specs=[pl.BlockSpec((DEC_SEQ, D_MODEL), seq_rows), cache_spec, cache_spec],
        out_shape=[jax.ShapeDtypeStruct((DEC_BATCH * DEC_SEQ, D_MODEL), F32),
                   jax.ShapeDtypeStruct((DEC_BATCH, WINDOW, D_KV), F32),
                   jax.ShapeDtypeStruct((DEC_BATCH, WINDOW, D_KV), F32)],
        scratch_shapes=[pltpu.VMEM((WINDOW, D_KV), F32)],
        compiler_params=_params(1),
        name="swa_sample",
    )(sink, q, k_new, v_new, k_cache, v_cache)


def _suffix_matrix():
    j = jnp.arange(2 * LANES)[:, None] % LANES
    s = jnp.arange(2 * LANES)[None, :]
    return ((s >= LANES) | (j > s)).astype(BF16)


def _sb_block(q, k, v, bias, u, carry, acc, mask):
    z = _dot_nt(q, k) + bias
    sp = jnp.maximum(z, 0.0) + jnp.log1p(jnp.exp(-jnp.abs(z)))
    log_rest = -sp if mask is None else jnp.where(mask, -sp, 0.0)
    hi = log_rest.astype(BF16)
    lo = (log_rest - hi.astype(F32)).astype(BF16)
    sums = _dot(jnp.concatenate([hi, lo], axis=1), u)
    a = jnp.exp(z - sp + sums[:, :LANES] + carry)
    if mask is not None:
        a = jnp.where(mask, a, 0.0)
    return carry + sums[:, LANES:], acc + _dot(a.astype(BF16), v)


def _sb_prompt_body(bias_ref, q_ref, k_ref, v_ref, u_ref, o_ref):
    i = pl.program_id(1)
    rows = GROUP * LANES
    u = u_ref[...]
    t = lax.broadcasted_iota(jnp.int32, (rows, LANES), 0) % LANES
    s = lax.broadcasted_iota(jnp.int32, (rows, LANES), 1)
    diag_mask = s < t
    out_parts = []
    for kv in range(N_KV_HEADS):
        lt = slice((kv // 2) * LANES, (kv // 2 + 1) * LANES)
        qs, biases = [], []
        for g in range(GROUP):
            h = kv * GROUP + g
            qs.append(_q_for_kv_tile(q_ref[:, (h // 2) * LANES:(h // 2 + 1) * LANES], h % 2, kv % 2))
            biases.append(jnp.full((LANES, LANES), bias_ref[h], F32))
        q4 = jnp.concatenate(qs, axis=0).astype(BF16)
        bias = jnp.concatenate(biases, axis=0)

        def block(cb, carry, acc, mask):
            off = pl.multiple_of(cb * LANES, LANES)
            kc = k_ref[pl.ds(off, LANES), lt].astype(BF16)
            vc = v_ref[pl.ds(off, LANES), lt].astype(BF16)
            return _sb_block(q4, kc, vc, bias, u, carry, acc, mask)

        zeros = jnp.zeros((rows, LANES), F32)
        carry, acc = block(i, zeros, zeros, diag_mask)
        carry, acc = lax.fori_loop(0, i, lambda n, ca: block(i - 1 - n, ca[0], ca[1], None), (carry, acc))
        out_parts += [(acc[g * LANES:(g + 1) * LANES], kv % 2) for g in range(GROUP)]
    tiles = [_merge_heads(out_parts[2 * p][0], out_parts[2 * p + 1][0], out_parts[2 * p][1], out_parts[2 * p + 1][1])
             for p in range(N_HEADS // 2)]
    o_ref[...] = jnp.concatenate(tiles, axis=1).astype(o_ref.dtype)


def _sb_prompt_call(bias, q, k, v, u):
    nb = SEQ // LANES
    return pl.pallas_call(
        _sb_prompt_body,
        grid=(BATCH, nb),
        in_specs=[pl.BlockSpec(memory_space=pltpu.SMEM),
                  pl.BlockSpec((LANES, D_MODEL), lambda b, i: (b * nb + i, 0)),
                  pl.BlockSpec((SEQ, D_KV), lambda b, i: (b, 0)),
                  pl.BlockSpec((SEQ, D_KV), lambda b, i: (b, 0)),
                  pl.BlockSpec((2 * LANES, 2 * LANES), lambda b, i: (0, 0))],
        out_specs=pl.BlockSpec((LANES, D_MODEL), lambda b, i: (b * nb + i, 0)),
        out_shape=jax.ShapeDtypeStruct((BATCH * SEQ, D_MODEL), BF16),
        compiler_params=_params(2),
        name="sb_prompt",
    )(bias, q, k, v, u)


def _sb_sample_body(pt_ref, bias_ref, q_ref, kn_ref, vn_ref, u_ref, *refs):
    del pt_ref
    page_refs = refs[:2 * PAGES_PER_STEP]
    o_ref, qbd_scr, bias_scr, carry_scr, acc_scr, pad_scr = refs[2 * PAGES_PER_STEP:]
    c = pl.program_id(1)
    rows = N_HEADS * DEC_SEQ
    u = u_ref[...]

    @pl.when(c == 0)
    def _first():
        qbd = _block_diag_q(q_ref[...]).astype(BF16)
        bias = _per_row_heads(bias_ref, DEC_SEQ, LANES)
        qbd_scr[...] = qbd
        bias_scr[...] = bias

        def padded(x):
            pad_scr[...] = jnp.zeros(pad_scr.shape, F32)
            pad_scr[0:DEC_SEQ, :] = x
            return pad_scr[...].astype(BF16)

        t = lax.broadcasted_iota(jnp.int32, (rows, LANES), 0) % DEC_SEQ
        s = lax.broadcasted_iota(jnp.int32, (rows, LANES), 1)
        kn = padded(kn_ref[...])
        vn = padded(vn_ref[...])
        carry, acc = _sb_block(qbd, kn, vn, bias, u, jnp.zeros((rows, LANES), F32),
                               jnp.zeros((rows, D_KV), F32), s < t)
        carry_scr[...] = carry
        acc_scr[...] = acc

    qbd, bias = qbd_scr[...], bias_scr[...]
    carry, acc = carry_scr[...], acc_scr[...]
    for slot in range(PAGES_PER_STEP):
        kp = page_refs[slot][...].astype(BF16)
        vp = page_refs[PAGES_PER_STEP + slot][...].astype(BF16)
        carry, acc = _sb_block(qbd, kp, vp, bias, u, carry, acc, None)
    carry_scr[...] = carry
    acc_scr[...] = acc

    @pl.when(c == pl.num_programs(1) - 1)
    def _last():
        o_ref[...] = _unpack_block_diag(acc)


def _sb_sample_call(page_table, bias, q, k_new, v_new, u, pool_k, pool_v):
    seq_rows = lambda s, c, pt: (s, 0)

    def page_spec(slot):
        return pl.BlockSpec((None, PAGE_SIZE, D_KV),
                            lambda s, c, pt: (pt[s, N_PAGES - 1 - (c * PAGES_PER_STEP + slot)], 0, 0))

    rows = N_HEADS * DEC_SEQ
    grid_spec = pltpu.PrefetchScalarGridSpec(
        num_scalar_prefetch=1,
        grid=(DEC_BATCH, N_PAGES // PAGES_PER_STEP),
        in_specs=[pl.BlockSpec(memory_space=pltpu.SMEM),
                  pl.BlockSpec((DEC_SEQ, D_MODEL), seq_rows),
                  pl.BlockSpec((DEC_SEQ, D_KV), seq_rows),
                  pl.BlockSpec((DEC_SEQ, D_KV), seq_rows),
                  pl.BlockSpec((2 * LANES, 2 * LANES), lambda s, c, pt: (0, 0))]
                 + [page_spec(slot) for slot in range(PAGES_PER_STEP)] * 2,
        out_specs=pl.BlockSpec((DEC_SEQ, D_MODEL), seq_rows),
        scratch_shapes=[pltpu.VMEM((rows, D_KV), BF16),
                        pltpu.VMEM((rows, LANES), F32),
                        pltpu.VMEM((rows, LANES), F32),
                        pltpu.VMEM((rows, D_KV), F32),
                        pltpu.VMEM((PAGE_SIZE, D_KV), F32)],
    )
    return pl.pallas_call(
        _sb_sample_body,
        grid_spec=grid_spec,
        out_shape=jax.ShapeDtypeStruct((DEC_BATCH * DEC_SEQ, D_MODEL), F32),
        compiler_params=_params(2),
        name="sb_sample",
    )(page_table, bias, q, k_new, v_new, u, *([pool_k] * PAGES_PER_STEP), *([pool_v] * PAGES_PER_STEP))


def _ffn_body(*refs, sample, tiles_per_seq):
    if sample:
        (x_ref, o_ref, wo_ref, g_ref, wgu_ref, cw_ref, cb_ref, wd_ref, prev_ref,
         xo_ref, conv_ref, h_scr, acc_scr) = refs
    else:
        (x_ref, o_ref, wo_ref, g_ref, wgu_ref, cw_ref, cb_ref, wd_ref,
         xo_ref, conv_ref, h_scr, acc_scr, carry_scr) = refs
    i = pl.program_id(0)
    tm = x_ref.shape[0]
    x1 = x_ref[...] + _dot(o_ref[...].astype(BF16), wo_ref[...])
    h_scr[...] = _rmsnorm(x1, g_ref[...]).astype(BF16)
    acc_scr[...] = x1
    if not sample:
        @pl.when(i % tiles_per_seq == 0)
        def _sequence_start():
            carry_scr[...] = jnp.zeros(carry_scr.shape, F32)

    def chunk(c, _):
        gu = _dot(h_scr[...], wgu_ref[c])
        gate, up = gu[:, :FF_CHUNK], gu[:, FF_CHUNK:]
        cw = cw_ref[c]
        if sample:
            nseq = tm // DEC_SEQ
            prev = prev_ref[c]
            expand = lambda p: jnp.broadcast_to(p, (nseq, DEC_SEQ, FF_CHUNK)).reshape(tm, FF_CHUNK)
            p0, p1 = expand(prev[:, 0:1, :]), expand(prev[:, 1:2, :])
            t = lax.broadcasted_iota(jnp.int32, (tm, FF_CHUNK), 0) % DEC_SEQ
            g1 = jnp.where(t == 0, p1, pltpu.roll(gate, 1, axis=0))
            g2 = jnp.where(t == 0, p0, jnp.where(t == 1, p1, pltpu.roll(gate, 2, axis=0)))
            conv_ref[c] = gate.reshape(nseq, DEC_SEQ, FF_CHUNK)[:, DEC_SEQ - (CONV_W - 1):, :]
        else:
            prev8 = carry_scr[c]
            top = gate[0:SUBLANES]
            r8 = lax.broadcasted_iota(jnp.int32, (SUBLANES, FF_CHUNK), 0)
            top1 = jnp.where(r8 < 1, pltpu.roll(prev8, 1, axis=0), pltpu.roll(top, 1, axis=0))
            top2 = jnp.where(r8 < 2, pltpu.roll(prev8, 2, axis=0), pltpu.roll(top, 2, axis=0))
            g1 = jnp.concatenate([top1, pltpu.roll(gate, 1, axis=0)[SUBLANES:]], axis=0)
            g2 = jnp.concatenate([top2, pltpu.roll(gate, 2, axis=0)[SUBLANES:]], axis=0)
            last8 = gate[tm - SUBLANES:]
            carry_scr[c] = last8
            conv_ref[c, 0] = last8
        gc = cb_ref[c] + cw[0:1] * g2 + cw[1:2] * g1 + cw[2:3] * gate
        act = gc / (1.0 + jnp.exp(-gc)) * up
        acc_scr[...] += _dot(act.astype(BF16), wd_ref[c])
        return 0

    lax.fori_loop(0, N_FF_CHUNKS, chunk, 0)
    xo_ref[...] = acc_scr[...]


def _ffn_call(x, o, wo, g, wgu, cw, cb, wd, prev, name):
    t = x.shape[0]
    sample = prev is not None
    tm = TOKEN_TILE
    tiles_per_seq = SEQ // tm
    const3 = lambda i: (0, 0, 0)
    in_specs = [pl.BlockSpec((tm, D_MODEL), lambda i: (i, 0)),
                pl.BlockSpec((tm, D_MODEL), lambda i: (i, 0)),
                _resident((D_MODEL, D_MODEL), lambda i: (0, 0)),
                pl.BlockSpec((1, D_MODEL), lambda i: (0, 0)),
                _resident((N_FF_CHUNKS, D_MODEL, 2 * FF_CHUNK), const3),
                pl.BlockSpec((N_FF_CHUNKS, CONV_W, FF_CHUNK), const3),
                pl.BlockSpec((N_FF_CHUNKS, 1, FF_CHUNK), const3),
                _resident((N_FF_CHUNKS, FF_CHUNK, D_MODEL), const3)]
    args = [x, o, wo, g.reshape(1, D_MODEL), wgu, cw, cb, wd]
    scratch = [pltpu.VMEM((tm, D_MODEL), BF16), pltpu.VMEM((tm, D_MODEL), F32)]
    if sample:
        nseq = tm // DEC_SEQ
        in_specs.append(pl.BlockSpec((N_FF_CHUNKS, nseq, CONV_W - 1, FF_CHUNK), lambda i: (0, i, 0, 0)))
        args.append(prev)
        conv_spec = pl.BlockSpec((N_FF_CHUNKS, nseq, CONV_W - 1, FF_CHUNK), lambda i: (0, i, 0, 0))
        conv_shape = jax.ShapeDtypeStruct((N_FF_CHUNKS, DEC_BATCH, CONV_W - 1, FF_CHUNK), F32)
    else:
        conv_spec = pl.BlockSpec((N_FF_CHUNKS, 1, SUBLANES, FF_CHUNK), lambda i: (0, i // tiles_per_seq, 0, 0))
        conv_shape = jax.ShapeDtypeStruct((N_FF_CHUNKS, BATCH, SUBLANES, FF_CHUNK), F32)
        scratch.append(pltpu.VMEM((N_FF_CHUNKS, SUBLANES, FF_CHUNK), F32))
    return pl.pallas_call(
        functools.partial(_ffn_body, sample=sample, tiles_per_seq=tiles_per_seq),
        grid=(t // tm,),
        in_specs=in_specs,
        out_specs=[pl.BlockSpec((tm, D_MODEL), lambda i: (i, 0)), conv_spec],
        out_shape=[jax.ShapeDtypeStruct((t, D_MODEL), F32), conv_shape],
        scratch_shapes=scratch,
        compiler_params=_params(1),
        name=name,
    )(*args)


def _norm_body(x_ref, g_ref, o_ref):
    o_ref[...] = _rmsnorm(x_ref[...], g_ref[...])


def _norm_call(x, g, name):
    t = x.shape[0]
    tm = TOKEN_TILE
    return pl.pallas_call(
        _norm_body,
        grid=(t // tm,),
        in_specs=[pl.BlockSpec((tm, D_MODEL), lambda i: (i, 0)), pl.BlockSpec((1, D_MODEL), lambda i: (0, 0))],
        out_specs=pl.BlockSpec((tm, D_MODEL), lambda i: (i, 0)),
        out_shape=jax.ShapeDtypeStruct((t, D_MODEL), F32),
        compiler_params=_params(1),
        name=name,
    )(x, g.reshape(1, D_MODEL))


def _chunk_cols(w):
    w = w.reshape(w.shape[:-1] + (N_FF_CHUNKS, FF_CHUNK))
    return jnp.moveaxis(w, -2, 0)


def _unchunk_cols(w):
    w = jnp.moveaxis(w, 0, -2)
    return w.reshape(w.shape[:-2] + (D_FF,))


def kernel(x_prompt, x_sample, cache_swa_k, cache_swa_v, cache_sb_k, cache_sb_v, state_conv, page_table,
           norm_mix, norm_ffn, norm_final, w_qkv_a, sinks_a, w_o_a, w_qkv_b, bias_b, w_o_b,
           w_in_ffn, conv_w, conv_b, w_down):
    xp = x_prompt.reshape(BATCH * SEQ, D_MODEL)
    xs = x_sample.reshape(DEC_BATCH * DEC_SEQ, D_MODEL)
    rope_p = _rope_tables(jnp.arange(SEQ))
    rope_s = _rope_tables(PAST_LEN + jnp.arange(TOKEN_TILE) % DEC_SEQ)
    u = _suffix_matrix()
    n_pool = cache_sb_k.shape[1]

    swa_kp, swa_vp, swa_ks, swa_vs = [], [], [], []
    sb_kp, sb_vp, sb_ks, sb_vs = [], [], [], []
    conv_p, conv_s = [], []
    for i in range(DEPTH):
        j = i // 2
        swa = i % 2 == 0
        w_qkv = (w_qkv_a if swa else w_qkv_b)[j].astype(BF16)
        w_o = (w_o_a if swa else w_o_b)[j].astype(BF16)
        qp, kp, vp = _qkv_call(xp, norm_mix[i], w_qkv, rope_p if swa else None, SEQ // TOKEN_TILE, f"qkv_prompt_{i}")
        qs, ks, vs = _qkv_call(xs, norm_mix[i], w_qkv, rope_s if swa else None, 1, f"qkv_sample_{i}")
        if swa:
            op = _swa_prompt_call(sinks_a[j], qp, kp, vp)
            os_, kw, vw = _swa_sample_call(sinks_a[j], qs, ks, vs,
                                           cache_swa_k[j].reshape(DEC_BATCH, WINDOW, D_KV),
                                           cache_swa_v[j].reshape(DEC_BATCH, WINDOW, D_KV))
            swa_kp.append(kp.reshape(BATCH, SEQ, N_KV_HEADS, HEAD_DIM)[:, SEQ - WINDOW:])
            swa_vp.append(vp.reshape(BATCH, SEQ, N_KV_HEADS, HEAD_DIM)[:, SEQ - WINDOW:])
            swa_ks.append(kw.reshape(DEC_BATCH, WINDOW, N_KV_HEADS, HEAD_DIM))
            swa_vs.append(vw.reshape(DEC_BATCH, WINDOW, N_KV_HEADS, HEAD_DIM))
        else:
            op = _sb_prompt_call(bias_b[j], qp, kp, vp, u)
            os_ = _sb_sample_call(page_table, bias_b[j], qs, ks, vs, u,
                                  cache_sb_k[j].reshape(n_pool, PAGE_SIZE, D_KV),
                                  cache_sb_v[j].reshape(n_pool, PAGE_SIZE, D_KV))
            sb_kp.append(kp.reshape(BATCH, SEQ, N_KV_HEADS, HEAD_DIM))
            sb_vp.append(vp.reshape(BATCH, SEQ, N_KV_HEADS, HEAD_DIM))
            sb_ks.append(ks.reshape(DEC_BATCH, DEC_SEQ, N_KV_HEADS, HEAD_DIM))
            sb_vs.append(vs.reshape(DEC_BATCH, DEC_SEQ, N_KV_HEADS, HEAD_DIM))
        wgu = jnp.concatenate([_chunk_cols(w_in_ffn[i][:, :D_FF]), _chunk_cols(w_in_ffn[i][:, D_FF:])],
                              axis=-1).astype(BF16)
        wd = w_down[i].reshape(N_FF_CHUNKS, FF_CHUNK, D_MODEL).astype(BF16)
        cw = _chunk_cols(conv_w[i])
        cb = _chunk_cols(conv_b[i][None, :])
        xp, cp = _ffn_call(xp, op, w_o, norm_ffn[i], wgu, cw, cb, wd, None, f"ffn_prompt_{i}")
        xs, cs = _ffn_call(xs, os_, w_o, norm_ffn[i], wgu, cw, cb, wd, _chunk_cols(state_conv[i]), f"ffn_sample_{i}")
        conv_p.append(_unchunk_cols(cp)[:, SUBLANES - (CONV_W - 1):])
        conv_s.append(_unchunk_cols(cs))
    y_prompt = _norm_call(xp, norm_final, "norm_prompt").reshape(BATCH, SEQ, D_MODEL)
    y_sample = _norm_call(xs, norm_final, "norm_sample").reshape(DEC_BATCH, DEC_SEQ, D_MODEL)
    return (y_prompt, y_sample,
            jnp.stack(swa_kp), jnp.stack(swa_vp), jnp.stack(sb_kp), jnp.stack(sb_vp), jnp.stack(conv_p),
            jnp.stack(swa_ks), jnp.stack(swa_vs), jnp.stack(sb_ks), jnp.stack(sb_vs), jnp.stack(conv_s))
```

```python
import functools

import jax
import jax.numpy as jnp
from jax import lax
from jax.experimental import pallas as pl
from jax.experimental.pallas import tpu as pltpu

F32 = jnp.float32
BF16 = jnp.bfloat16

D_MODEL = 1024
HEAD_DIM = 64
N_HEADS = 16
N_KV_HEADS = 4
GROUP = N_HEADS // N_KV_HEADS
D_KV = N_KV_HEADS * HEAD_DIM
DEC_SEQ = 8
WINDOW = 128
PAGE_SIZE = 128
ROPE_THETA = 10000.0
D_FF = 2816
CONV_W = 3
RMS_EPS = 1e-6
ATTN_SCALE = HEAD_DIM ** -0.5

LANES = 128
SUBLANES = 8
D_QX = N_HEADS * LANES
FF_CHUNK = 256
N_FF_CHUNKS = D_FF // FF_CHUNK
TOKEN_TILE = 512
SB_BLOCK = 2 * LANES
PAGES_PER_STEP = 16
VMEM_LIMIT = 56 * 1024 * 1024

assert D_FF % FF_CHUNK == 0 and TOKEN_TILE % SB_BLOCK == 0
assert DEC_SEQ == SUBLANES and WINDOW == LANES and PAGE_SIZE == LANES and 2 * HEAD_DIM == LANES


def _params(n_axes):
    return pltpu.CompilerParams(dimension_semantics=("arbitrary",) * n_axes,
                                vmem_limit_bytes=VMEM_LIMIT)


def _resident(shape, index_map):
    return pl.BlockSpec(shape, index_map, pipeline_mode=pl.Buffered(1))


def _rmsnorm(x, g):
    r = lax.rsqrt(jnp.mean(x * x, axis=-1, keepdims=True) + RMS_EPS)
    return x * r * g


def _dot(a, b):
    return jnp.dot(a, b, preferred_element_type=F32)


def _dot_nt(a, b):
    return lax.dot_general(a, b, (((1,), (1,)), ((), ())), preferred_element_type=F32)


def _kv_pair(kv):
    return kv // 2, kv % 2


def _rope_tile(x, cosf, sinf):
    lane = lax.broadcasted_iota(jnp.int32, x.shape, 1)
    first = (lane % HEAD_DIM) < (HEAD_DIM // 2)
    other = jnp.where(first, pltpu.roll(x, LANES - HEAD_DIM // 2, axis=1),
                      pltpu.roll(x, HEAD_DIM // 2, axis=1))
    return x * cosf + other * sinf


def _rope_rows(x, cos_t, sin_t):
    half = HEAD_DIM // 2
    out = []
    for kv in range(N_KV_HEADS):
        x1 = x[kv * HEAD_DIM:kv * HEAD_DIM + half]
        x2 = x[kv * HEAD_DIM + half:(kv + 1) * HEAD_DIM]
        out += [x1 * cos_t - x2 * sin_t, x2 * cos_t + x1 * sin_t]
    return jnp.concatenate(out, axis=0)


def _qkv_prompt_body(*refs, rope, blocked):
    refs = list(refs)
    x_ref, g_ref, wq_ref, wk_ref, wv_ref = refs[:5]
    refs = refs[5:]
    if rope:
        cos_ref, sin_ref, cos_t_ref, sin_t_ref = refs[:4]
        refs = refs[4:]
    q_ref, k_ref, v_ref = refs[:3]
    h = _rmsnorm(x_ref[...], g_ref[...]).astype(BF16)
    tm = h.shape[0]
    q = _dot(h, wq_ref[...])
    k_t = _dot_nt(wk_ref[...], h)
    v_t = _dot_nt(wv_ref[...], h)
    if rope:
        cosf, sinf = cos_ref[...], sin_ref[...]
        for j in range(N_HEADS):
            sl = slice(j * LANES, (j + 1) * LANES)
            q_ref[:, sl] = _rope_tile(q[:, sl], cosf, sinf).astype(q_ref.dtype)
        k_t = _rope_rows(k_t, cos_t_ref[...], sin_t_ref[...])
    else:
        q_ref[...] = q.astype(q_ref.dtype)
    k_ref[...] = k_t.reshape(N_KV_HEADS, HEAD_DIM, tm)
    v_ref[...] = v_t.reshape(N_KV_HEADS, HEAD_DIM, tm)
    if blocked:
        kb_ref, vb_ref = refs[3:5]
        for c in range(tm // SB_BLOCK):
            sl = slice(c * SB_BLOCK, (c + 1) * SB_BLOCK)
            kb_ref[c] = k_t[:, sl].reshape(N_KV_HEADS, HEAD_DIM, SB_BLOCK)
            vb_ref[c] = v_t[:, sl].reshape(N_KV_HEADS, HEAD_DIM, SB_BLOCK)


def _qkv_prompt_call(x, g, wqx, wk_t, wv_t, tables, batch, seq, blocked, name):
    tm = TOKEN_TILE
    tps = seq // tm
    rope = tables is not None
    row = lambda i: (i, 0)
    const = lambda i: (0, 0)
    in_specs = [pl.BlockSpec((tm, D_MODEL), row),
                pl.BlockSpec((1, D_MODEL), const),
                _resident((D_MODEL, D_QX), const),
                _resident((D_KV, D_MODEL), const),
                _resident((D_KV, D_MODEL), const)]
    args = [x, g.reshape(1, D_MODEL), wqx, wk_t, wv_t]
    if rope:
        cosf, sinf, cos_t, sin_t = tables
        in_specs += [pl.BlockSpec((tm, LANES), lambda i: (i % tps, 0))] * 2
        in_specs += [pl.BlockSpec((HEAD_DIM // 2, tm), lambda i: (0, i % tps))] * 2
        args += [cosf, sinf, cos_t, sin_t]
    kv_spec = pl.BlockSpec((None, N_KV_HEADS, HEAD_DIM, tm), lambda i: (i // tps, 0, 0, i % tps))
    kv_shape = jax.ShapeDtypeStruct((batch, N_KV_HEADS, HEAD_DIM, seq), F32)
    out_specs = [pl.BlockSpec((tm, D_QX), row), kv_spec, kv_spec]
    out_shape = [jax.ShapeDtypeStruct((batch * seq, D_QX), BF16), kv_shape, kv_shape]
    if blocked:
        nb = tm // SB_BLOCK
        b_spec = pl.BlockSpec((None, nb, N_KV_HEADS, HEAD_DIM, SB_BLOCK), lambda i: (i // tps, i % tps, 0, 0, 0))
        b_shape = jax.ShapeDtypeStruct((batch, seq // SB_BLOCK, N_KV_HEADS, HEAD_DIM, SB_BLOCK), F32)
        out_specs += [b_spec, b_spec]
        out_shape += [b_shape, b_shape]
    return pl.pallas_call(
        functools.partial(_qkv_prompt_body, rope=rope, blocked=blocked),
        grid=(batch * seq // tm,),
        in_specs=in_specs, out_specs=out_specs, out_shape=out_shape,
        compiler_params=_params(1), name=name,
    )(*args)


def _qkv_sample_body(*refs, rope):
    if rope:
        x_ref, g_ref, wq_ref, wkv_ref, cos_ref, sin_ref, q_ref, k_ref, v_ref = refs
    else:
        x_ref, g_ref, wq_ref, wkv_ref, q_ref, k_ref, v_ref = refs
    h = _rmsnorm(x_ref[...], g_ref[...]).astype(BF16)
    q = _dot(h, wq_ref[...])
    kv = _dot(h, wkv_ref[...])
    if rope:
        cosf, sinf = cos_ref[...], sin_ref[...]
        for j in range(N_HEADS):
            sl = slice(j * LANES, (j + 1) * LANES)
            q_ref[:, sl] = _rope_tile(q[:, sl], cosf, sinf)
        for j in range(D_KV // LANES):
            sl = slice(j * LANES, (j + 1) * LANES)
            k_ref[:, sl] = _rope_tile(kv[:, sl], cosf, sinf)
    else:
        q_ref[...] = q
        k_ref[...] = kv[:, :D_KV]
    v_ref[...] = kv[:, D_KV:]


def _qkv_sample_call(x, g, wqx, wkv, tables, name):
    t = x.shape[0]
    tm = min(TOKEN_TILE, t)
    rope = tables is not None
    row = lambda i: (i, 0)
    const = lambda i: (0, 0)
    in_specs = [pl.BlockSpec((tm, D_MODEL), row),
                pl.BlockSpec((1, D_MODEL), const),
                _resident((D_MODEL, D_QX), const),
                _resident((D_MODEL, 2 * D_KV), const)]
    args = [x, g.reshape(1, D_MODEL), wqx, wkv]
    if rope:
        in_specs += [pl.BlockSpec((tm, LANES), const)] * 2
        args += list(tables)
    return pl.pallas_call(
        functools.partial(_qkv_sample_body, rope=rope),
        grid=(t // tm,),
        in_specs=in_specs,
        out_specs=[pl.BlockSpec((tm, D_QX), row), pl.BlockSpec((tm, D_KV), row), pl.BlockSpec((tm, D_KV), row)],
        out_shape=[jax.ShapeDtypeStruct((t, D_QX), F32),
                   jax.ShapeDtypeStruct((t, D_KV), F32),
                   jax.ShapeDtypeStruct((t, D_KV), F32)],
        compiler_params=_params(1), name=name,
    )(*args)


def _rope_tables(pos):
    inv_freq = ROPE_THETA ** (-jnp.arange(0, HEAD_DIM, 2, dtype=F32) / HEAD_DIM)
    ang = pos.astype(F32)[:, None] * inv_freq[None, :]
    cos, sin = jnp.cos(ang), jnp.sin(ang)
    cosf = jnp.concatenate([cos, cos, cos, cos], axis=1)
    sinf = jnp.concatenate([-sin, sin, -sin, sin], axis=1)
    return cosf, sinf, cos.T, sin.T


def _expand_q_weight(w_q):
    w = (w_q * ATTN_SCALE).reshape(D_MODEL, N_HEADS, 1, HEAD_DIM)
    half = ((jnp.arange(N_HEADS) // GROUP) % 2)[None, :, None, None]
    w = jnp.where(half == jnp.arange(2)[None, None, :, None], w, 0.0)
    return w.reshape(D_MODEL, D_QX).astype(BF16)


def _half_mask(shape, half):
    lane = lax.broadcasted_iota(jnp.int32, shape, 1)
    return (lane >= HEAD_DIM) if half else (lane < HEAD_DIM)


def _merge_heads(even, odd, even_half, odd_half):
    if even_half != 0:
        even = pltpu.roll(even, HEAD_DIM, axis=1)
    if odd_half != 1:
        odd = pltpu.roll(odd, HEAD_DIM, axis=1)
    return jnp.where(_half_mask(even.shape, 0), even, odd)


def _merge_all_heads(parts):
    tiles = [_merge_heads(parts[2 * p][0], parts[2 * p + 1][0], parts[2 * p][1], parts[2 * p + 1][1])
             for p in range(N_HEADS // 2)]
    return jnp.concatenate(tiles, axis=1)


def _block_diag_q(qx):
    zeros = jnp.zeros((SUBLANES, LANES), qx.dtype)
    rows = []
    for h in range(N_HEADS):
        pair, _ = _kv_pair(h // GROUP)
        tile = qx[:, h * LANES:(h + 1) * LANES]
        rows.append(jnp.concatenate([tile, zeros] if pair == 0 else [zeros, tile], axis=1))
    return jnp.concatenate(rows, axis=0)


def _unpack_block_diag(acc):
    parts = []
    for h in range(N_HEADS):
        pair, half = _kv_pair(h // GROUP)
        parts.append((acc[h * SUBLANES:(h + 1) * SUBLANES, pair * LANES:(pair + 1) * LANES], half))
    return _merge_all_heads(parts)


def _per_row_heads(scalar_ref, rows_per_head, width):
    return jnp.concatenate([jnp.full((rows_per_head, width), scalar_ref[h], F32) for h in range(N_HEADS)], axis=0)


def _pad_rows(pad_scr, x):
    pad_scr[...] = jnp.zeros(pad_scr.shape, F32)
    pad_scr[0:DEC_SEQ, :] = x
    return pad_scr[...]


def _swa_prompt_body(sink_ref, q_ref, kp_ref, kc_ref, vp_ref, vc_ref, o_ref):
    j = pl.program_id(1)
    rows = GROUP * WINDOW
    a = lax.broadcasted_iota(jnp.int32, (rows, 2 * WINDOW), 0) % WINDOW
    c = lax.broadcasted_iota(jnp.int32, (rows, 2 * WINDOW), 1)
    mask = (c >= a) & (c <= a + WINDOW) & ((j > 0) | (c >= WINDOW))
    parts = []
    for kv in range(N_KV_HEADS):
        pair, half = _kv_pair(kv)
        band = lambda prev, cur: jnp.concatenate(
            [prev[2 * pair:2 * pair + 2].reshape(LANES, WINDOW), cur[2 * pair:2 * pair + 2].reshape(LANES, WINDOW)],
            axis=1).astype(BF16)
        k_t = band(kp_ref, kc_ref)
        v_t = band(vp_ref, vc_ref)
        heads = range(kv * GROUP, (kv + 1) * GROUP)
        q4 = jnp.concatenate([q_ref[:, h * LANES:(h + 1) * LANES] for h in heads], axis=0)
        sink = jnp.concatenate([jnp.full((WINDOW, 1), sink_ref[h], F32) for h in heads], axis=0)
        z = jnp.where(mask, _dot(q4, k_t), -jnp.inf)
        m = jnp.maximum(jnp.max(z, axis=-1, keepdims=True), sink)
        p = jnp.exp(z - m)
        denom = jnp.sum(p, axis=-1, keepdims=True) + jnp.exp(sink - m)
        o4 = _dot_nt(p.astype(BF16), v_t) / denom
        parts += [(o4[g * WINDOW:(g + 1) * WINDOW], half) for g in range(GROUP)]
    o_ref[...] = _merge_all_heads(parts).astype(o_ref.dtype)


def _swa_prompt_call(sink, qx, k_t, v_t):
    batch, _, _, seq = k_t.shape
    nb = seq // WINDOW
    kv_spec = lambda f: pl.BlockSpec((None, N_KV_HEADS, HEAD_DIM, WINDOW), f)
    cur = lambda b, j: (b, 0, 0, j)
    prev = lambda b, j: (b, 0, 0, jnp.maximum(j - 1, 0))
    return pl.pallas_call(
        _swa_prompt_body,
        grid=(batch, nb),
        in_specs=[pl.BlockSpec(memory_space=pltpu.SMEM),
                  pl.BlockSpec((WINDOW, D_QX), lambda b, j: (b * nb + j, 0)),
                  kv_spec(prev), kv_spec(cur), kv_spec(prev), kv_spec(cur)],
        out_specs=pl.BlockSpec((WINDOW, D_MODEL), lambda b, j: (b * nb + j, 0)),
        out_shape=jax.ShapeDtypeStruct((batch * seq, D_MODEL), BF16),
        compiler_params=_params(2), name="swa_prompt",
    )(sink, qx, k_t, k_t, v_t, v_t)


def _swa_sample_body(sink_ref, q_ref, kn_ref, vn_ref, kc_ref, vc_ref, o_ref, kw_ref, vw_ref, pad_scr):
    rows = N_HEADS * DEC_SEQ
    t = lax.broadcasted_iota(jnp.int32, (rows, WINDOW), 0) % DEC_SEQ
    c = lax.broadcasted_iota(jnp.int32, (rows, WINDOW), 1)
    kc_t = kc_ref[...].reshape(D_KV, WINDOW)
    vc_t = vc_ref[...].reshape(D_KV, WINDOW)
    kn = _pad_rows(pad_scr, kn_ref[...])
    vn = _pad_rows(pad_scr, vn_ref[...])

    lane = lax.broadcasted_iota(jnp.int32, (D_KV, WINDOW), 1)
    keep = WINDOW - DEC_SEQ
    shift = lambda x: pltpu.roll(x, keep, axis=1)
    kw_ref[...] = jnp.where(lane < keep, shift(kc_t), shift(kn.T)).reshape(N_KV_HEADS, HEAD_DIM, WINDOW)
    vw_ref[...] = jnp.where(lane < keep, shift(vc_t), shift(vn.T)).reshape(N_KV_HEADS, HEAD_DIM, WINDOW)

    qbd = _block_diag_q(q_ref[...]).astype(BF16)
    sink = _per_row_heads(sink_ref, DEC_SEQ, 1)
    zc = jnp.where(c >= t, _dot(qbd, kc_t.astype(BF16)), -jnp.inf)
    zn = jnp.where(c <= t, _dot_nt(qbd, kn.astype(BF16)), -jnp.inf)
    m = jnp.maximum(jnp.maximum(jnp.max(zc, axis=-1, keepdims=True), jnp.max(zn, axis=-1, keepdims=True)), sink)
    pc = jnp.exp(zc - m)
    pn = jnp.exp(zn - m)
    denom = jnp.sum(pc, axis=-1, keepdims=True) + jnp.sum(pn, axis=-1, keepdims=True) + jnp.exp(sink - m)
    acc = (_dot_nt(pc.astype(BF16), vc_t.astype(BF16)) + _dot(pn.astype(BF16), vn.astype(BF16))) / denom
    o_ref[...] = _unpack_block_diag(acc)


def _swa_sample_call(sink, qx, k_new, v_new, k_cache_t, v_cache_t, layer):
    n = k_cache_t.shape[1]
    seq_rows = lambda s: (s, 0)
    cache_in = pl.BlockSpec((None, None, N_KV_HEADS, HEAD_DIM, WINDOW), lambda s: (layer, s, 0, 0, 0))
    cache_out = pl.BlockSpec((None, N_KV_HEADS, HEAD_DIM, WINDOW), lambda s: (s, 0, 0, 0))
    win_shape = jax.ShapeDtypeStruct((n, N_KV_HEADS, HEAD_DIM, WINDOW), F32)
    return pl.pallas_call(
        _swa_sample_body,
        grid=(n,),
        in_specs=[pl.BlockSpec(memory_space=pltpu.SMEM),
                  pl.BlockSpec((DEC_SEQ, D_QX), seq_rows),
                  pl.BlockSpec((DEC_SEQ, D_KV), seq_rows),
                  pl.BlockSpec((DEC_SEQ, D_KV), seq_rows),
                  cache_in, cache_in],
        out_specs=[pl.BlockSpec((DEC_SEQ, D_MODEL), seq_rows), cache_out, cache_out],
        out_shape=[jax.ShapeDtypeStruct((n * DEC_SEQ, D_MODEL), F32), win_shape, win_shape],
        scratch_shapes=[pltpu.VMEM((WINDOW, D_KV), F32)],
        compiler_params=_params(1), name="swa_sample",
    )(sink, qx, k_new, v_new, k_cache_t, v_cache_t)


def _suffix_matrix():
    j = jnp.arange(2 * LANES)[:, None] % LANES
    s = jnp.arange(2 * LANES)[None, :]
    return ((s >= LANES) | (j > s)).astype(BF16)


def _softplus(z):
    return jnp.maximum(z, 0.0) + jnp.log(1.0 + jnp.exp(-jnp.abs(z)))


def _hi_lo(x):
    hi = x.astype(BF16)
    lo = (x - hi.astype(F32)).astype(BF16)
    return jnp.concatenate([hi, lo], axis=1)


def _sb_blocks(z_blocks, masks, u, carry):
    m = z_blocks[0].shape[0]
    sps = [_softplus(z) for z in z_blocks]
    log_rest = [-sp if mk is None else jnp.where(mk, -sp, 0.0) for sp, mk in zip(sps, masks)]
    sums = _dot(jnp.concatenate([_hi_lo(x) for x in log_rest], axis=0), u)
    weights = []
    for n, (z, sp, mk) in enumerate(zip(z_blocks, sps, masks)):
        s = sums[n * m:(n + 1) * m]
        a = jnp.exp(z - sp + s[:, :LANES] + carry)
        weights.append(a if mk is None else jnp.where(mk, a, 0.0))
        carry = carry + s[:, LANES:]
    return weights, carry


def _sb_prompt_body(bias_ref, q_ref, k_ref, v_ref, u_ref, o_ref, carry_scr, acc_scr):
    i = pl.program_id(1)
    rows = GROUP * SB_BLOCK
    sub = SB_BLOCK // LANES
    u = u_ref[...]
    t = lax.broadcasted_iota(jnp.int32, (rows, LANES), 0) % SB_BLOCK
    s = lax.broadcasted_iota(jnp.int32, (rows, LANES), 1)
    diag_masks = [(s + n * LANES) < t for n in reversed(range(sub))]

    def block(c, masks, first):
        for kv in range(N_KV_HEADS):
            pair, _ = _kv_pair(kv)
            heads = range(kv * GROUP, (kv + 1) * GROUP)
            q4 = jnp.concatenate([q_ref[:, h * LANES:(h + 1) * LANES] for h in heads], axis=0)
            bias = jnp.concatenate([jnp.full((SB_BLOCK, LANES), bias_ref[h], F32) for h in heads], axis=0)
            k_t = k_ref[c, 2 * pair:2 * pair + 2].reshape(LANES, SB_BLOCK).astype(BF16)
            v_t = v_ref[c, 2 * pair:2 * pair + 2].reshape(LANES, SB_BLOCK).astype(BF16)
            z = _dot(q4, k_t)
            zs = [z[:, n * LANES:(n + 1) * LANES] + bias for n in reversed(range(sub))]
            carry = jnp.zeros((rows, LANES), F32) if first else carry_scr[kv]
            weights, carry = _sb_blocks(zs, masks, u, carry)
            a = jnp.concatenate(list(reversed(weights)), axis=1).astype(BF16)
            pv = _dot_nt(a, v_t)
            acc_scr[kv] = pv if first else acc_scr[kv] + pv
            carry_scr[kv] = carry

    block(i, diag_masks, True)
    lax.fori_loop(0, i, lambda n, _: (block(i - 1 - n, [None] * sub, False), 0)[1], 0)
    parts = []
    for kv in range(N_KV_HEADS):
        acc = acc_scr[kv]
        parts += [(acc[g * SB_BLOCK:(g + 1) * SB_BLOCK], _kv_pair(kv)[1]) for g in range(GROUP)]
    o_ref[...] = _merge_all_heads(parts).astype(o_ref.dtype)


def _sb_prompt_call(bias, qx, k_blocks, v_blocks, u):
    batch, nb = k_blocks.shape[:2]
    rows = GROUP * SB_BLOCK
    kv_spec = pl.BlockSpec((None, nb, N_KV_HEADS, HEAD_DIM, SB_BLOCK), lambda b, i: (b, 0, 0, 0, 0))
    return pl.pallas_call(
        _sb_prompt_body,
        grid=(batch, nb),
        in_specs=[pl.BlockSpec(memory_space=pltpu.SMEM),
                  pl.BlockSpec((SB_BLOCK, D_QX), lambda b, i: (b * nb + i, 0)),
                  kv_spec, kv_spec,
                  pl.BlockSpec((2 * LANES, 2 * LANES), lambda b, i: (0, 0))],
        out_specs=pl.BlockSpec((SB_BLOCK, D_MODEL), lambda b, i: (b * nb + i, 0)),
        out_shape=jax.ShapeDtypeStruct((batch * nb * SB_BLOCK, D_MODEL), BF16),
        scratch_shapes=[pltpu.VMEM((N_KV_HEADS, rows, LANES), F32),
                        pltpu.VMEM((N_KV_HEADS, rows, LANES), F32)],
        compiler_params=_params(2), name="sb_prompt",
    )(bias, qx, k_blocks, v_blocks, u)


def _sb_sample_body(pt_ref, bias_ref, q_ref, kn_ref, vn_ref, u_ref, *refs):
    del pt_ref
    page_refs = refs[:2 * PAGES_PER_STEP]
    o_ref, qbd_scr, bias_scr, carry_scr, acc_scr, pad_scr = refs[2 * PAGES_PER_STEP:]
    c = pl.program_id(1)
    rows = N_HEADS * DEC_SEQ
    u = u_ref[...]

    @pl.when(c == 0)
    def _new_tokens():
        qbd = _block_diag_q(q_ref[...]).astype(BF16)
        bias = _per_row_heads(bias_ref, DEC_SEQ, LANES)
        qbd_scr[...] = qbd
        bias_scr[...] = bias
        kn = _pad_rows(pad_scr, kn_ref[...]).astype(BF16)
        vn = _pad_rows(pad_scr, vn_ref[...]).astype(BF16)
        t = lax.broadcasted_iota(jnp.int32, (rows, LANES), 0) % DEC_SEQ
        s = lax.broadcasted_iota(jnp.int32, (rows, LANES), 1)
        (a,), carry = _sb_blocks([_dot_nt(qbd, kn) + bias], [s < t], u, jnp.zeros((rows, LANES), F32))
        carry_scr[...] = carry
        acc_scr[...] = _dot(a.astype(BF16), vn)

    qbd, bias = qbd_scr[...], bias_scr[...]
    page = lambda r: r[...].reshape(D_KV, PAGE_SIZE).astype(BF16)
    k_t = jnp.concatenate([page(page_refs[slot]) for slot in range(PAGES_PER_STEP)], axis=1)
    v_t = jnp.concatenate([page(page_refs[PAGES_PER_STEP + slot]) for slot in range(PAGES_PER_STEP)], axis=1)
    z = _dot(qbd, k_t)
    zs = [z[:, n * LANES:(n + 1) * LANES] + bias for n in range(PAGES_PER_STEP)]
    weights, carry = _sb_blocks(zs, [None] * PAGES_PER_STEP, u, carry_scr[...])
    acc = acc_scr[...] + _dot_nt(jnp.concatenate(weights, axis=1).astype(BF16), v_t)
    carry_scr[...] = carry
    acc_scr[...] = acc

    @pl.when(c == pl.num_programs(1) - 1)
    def _last():
        o_ref[...] = _unpack_block_diag(acc)


def _sb_sample_call(page_table, bias, qx, k_new, v_new, u, pool_k_t, pool_v_t, layer):
    n, n_pages = page_table.shape
    assert n_pages % PAGES_PER_STEP == 0
    seq_rows = lambda s, c, pt: (s, 0)

    def page_spec(slot):
        return pl.BlockSpec((None, None, N_KV_HEADS, HEAD_DIM, PAGE_SIZE),
                            lambda s, c, pt: (layer, pt[s, n_pages - 1 - (c * PAGES_PER_STEP + slot)], 0, 0, 0))

    rows = N_HEADS * DEC_SEQ
    grid_spec = pltpu.PrefetchScalarGridSpec(
        num_scalar_prefetch=1,
        grid=(n, n_pages // PAGES_PER_STEP),
        in_specs=[pl.BlockSpec(memory_space=pltpu.SMEM),
                  pl.BlockSpec((DEC_SEQ, D_QX), seq_rows),
                  pl.BlockSpec((DEC_SEQ, D_KV), seq_rows),
                  pl.BlockSpec((DEC_SEQ, D_KV), seq_rows),
                  pl.BlockSpec((2 * LANES, 2 * LANES), lambda s, c, pt: (0, 0))]
                 + [page_spec(slot) for slot in range(PAGES_PER_STEP)] * 2,
        out_specs=pl.BlockSpec((DEC_SEQ, D_MODEL), seq_rows),
        scratch_shapes=[pltpu.VMEM((rows, D_KV), BF16),
                        pltpu.VMEM((rows, LANES), F32),
                        pltpu.VMEM((rows, LANES), F32),
                        pltpu.VMEM((rows, D_KV), F32),
                        pltpu.VMEM((PAGE_SIZE, D_KV), F32)],
    )
    return pl.pallas_call(
        _sb_sample_body,
        grid_spec=grid_spec,
        out_shape=jax.ShapeDtypeStruct((n * DEC_SEQ, D_MODEL), F32),
        compiler_params=_params(2), name="sb_sample",
    )(page_table, bias, qx, k_new, v_new, u, *([pool_k_t] * PAGES_PER_STEP), *([pool_v_t] * PAGES_PER_STEP))


def _ffn_body(*refs, sample, tiles_per_seq):
    if sample:
        (x_ref, o_ref, wo_ref, g_ref, wgu_ref, cw_ref, cb_ref, wd_ref, prev_ref,
         xo_ref, conv_ref, h_scr, acc_scr, act_scr) = refs
    else:
        (x_ref, o_ref, wo_ref, g_ref, wgu_ref, cw_ref, cb_ref, wd_ref,
         xo_ref, conv_ref, h_scr, acc_scr, act_scr, carry_scr) = refs
    i = pl.program_id(0)
    tm = x_ref.shape[0]
    x1 = x_ref[...] + _dot(o_ref[...].astype(BF16), wo_ref[...])
    h_scr[...] = _rmsnorm(x1, g_ref[...]).astype(BF16)
    acc_scr[...] = x1
    if not sample:
        @pl.when(i % tiles_per_seq == 0)
        def _sequence_start():
            carry_scr[...] = jnp.zeros(carry_scr.shape, F32)

    def activation(c):
        gu = _dot(h_scr[...], wgu_ref[c])
        gate, up = gu[:, :FF_CHUNK], gu[:, FF_CHUNK:]
        cw = cw_ref[c]
        if sample:
            nseq = tm // DEC_SEQ
            prev = prev_ref[c]
            expand = lambda p: jnp.broadcast_to(p, (nseq, DEC_SEQ, FF_CHUNK)).reshape(tm, FF_CHUNK)
            p0, p1 = expand(prev[:, 0:1, :]), expand(prev[:, 1:2, :])
            t = lax.broadcasted_iota(jnp.int32, (tm, FF_CHUNK), 0) % DEC_SEQ
            g1 = jnp.where(t == 0, p1, pltpu.roll(gate, 1, axis=0))
            g2 = jnp.where(t == 0, p0, jnp.where(t == 1, p1, pltpu.roll(gate, 2, axis=0)))
            conv_ref[c] = gate.reshape(nseq, DEC_SEQ, FF_CHUNK)[:, DEC_SEQ - (CONV_W - 1):, :]
        else:
            prev8 = carry_scr[c]
            top = gate[0:SUBLANES]
            r8 = lax.broadcasted_iota(jnp.int32, (SUBLANES, FF_CHUNK), 0)
            top1 = jnp.where(r8 < 1, pltpu.roll(prev8, 1, axis=0), pltpu.roll(top, 1, axis=0))
            top2 = jnp.where(r8 < 2, pltpu.roll(prev8, 2, axis=0), pltpu.roll(top, 2, axis=0))
            g1 = jnp.concatenate([top1, pltpu.roll(gate, 1, axis=0)[SUBLANES:]], axis=0)
            g2 = jnp.concatenate([top2, pltpu.roll(gate, 2, axis=0)[SUBLANES:]], axis=0)
            last8 = gate[tm - SUBLANES:]
            carry_scr[c] = last8
            conv_ref[c, 0] = last8
        gc = cb_ref[c] + cw[0:1] * g2 + cw[1:2] * g1 + cw[2:3] * gate
        return (gc / (1.0 + jnp.exp(-gc)) * up).astype(BF16)

    act_scr[...] = activation(0)

    def step(c, _):
        prev_act = act_scr[...]
        act = activation(c)
        acc_scr[...] += _dot(prev_act, wd_ref[c - 1])
        act_scr[...] = act
        return 0

    lax.fori_loop(1, N_FF_CHUNKS, step, 0)
    xo_ref[...] = acc_scr[...] + _dot(act_scr[...], wd_ref[N_FF_CHUNKS - 1])


def _ffn_call(x, o, wo, g, wgu, cw, cb, wd, prev, seq, name):
    t = x.shape[0]
    sample = prev is not None
    tm = min(TOKEN_TILE, t)
    tiles_per_seq = None if sample else seq // tm
    const3 = lambda i: (0, 0, 0)
    in_specs = [pl.BlockSpec((tm, D_MODEL), lambda i: (i, 0)),
                pl.BlockSpec((tm, D_MODEL), lambda i: (i, 0)),
                _resident((D_MODEL, D_MODEL), lambda i: (0, 0)),
                pl.BlockSpec((1, D_MODEL), lambda i: (0, 0)),
                _resident((N_FF_CHUNKS, D_MODEL, 2 * FF_CHUNK), const3),
                pl.BlockSpec((N_FF_CHUNKS, CONV_W, FF_CHUNK), const3),
                pl.BlockSpec((N_FF_CHUNKS, 1, FF_CHUNK), const3),
                _resident((N_FF_CHUNKS, FF_CHUNK, D_MODEL), const3)]
    args = [x, o, wo, g.reshape(1, D_MODEL), wgu, cw, cb, wd]
    scratch = [pltpu.VMEM((tm, D_MODEL), BF16), pltpu.VMEM((tm, D_MODEL), F32), pltpu.VMEM((tm, FF_CHUNK), BF16)]
    if sample:
        nseq = tm // DEC_SEQ
        state_spec = pl.BlockSpec((N_FF_CHUNKS, nseq, CONV_W - 1, FF_CHUNK), lambda i: (0, i, 0, 0))
        in_specs.append(state_spec)
        args.append(prev)
        conv_spec = state_spec
        conv_shape = jax.ShapeDtypeStruct((N_FF_CHUNKS, t // DEC_SEQ, CONV_W - 1, FF_CHUNK), F32)
    else:
        conv_spec = pl.BlockSpec((N_FF_CHUNKS, 1, SUBLANES, FF_CHUNK), lambda i: (0, i // tiles_per_seq, 0, 0))
        conv_shape = jax.ShapeDtypeStruct((N_FF_CHUNKS, t // seq, SUBLANES, FF_CHUNK), F32)
        scratch.append(pltpu.VMEM((N_FF_CHUNKS, SUBLANES, FF_CHUNK), F32))
    return pl.pallas_call(
        functools.partial(_ffn_body, sample=sample, tiles_per_seq=tiles_per_seq),
        grid=(t // tm,),
        in_specs=in_specs,
        out_specs=[pl.BlockSpec((tm, D_MODEL), lambda i: (i, 0)), conv_spec],
        out_shape=[jax.ShapeDtypeStruct((t, D_MODEL), F32), conv_shape],
        scratch_shapes=scratch,
        compiler_params=_params(1), name=name,
    )(*args)


def _norm_body(x_ref, g_ref, o_ref):
    o_ref[...] = _rmsnorm(x_ref[...], g_ref[...])


def _norm_call(x, g, name):
    t = x.shape[0]
    tm = min(TOKEN_TILE, t)
    return pl.pallas_call(
        _norm_body,
        grid=(t // tm,),
        in_specs=[pl.BlockSpec((tm, D_MODEL), lambda i: (i, 0)), pl.BlockSpec((1, D_MODEL), lambda i: (0, 0))],
        out_specs=pl.BlockSpec((tm, D_MODEL), lambda i: (i, 0)),
        out_shape=jax.ShapeDtypeStruct((t, D_MODEL), F32),
        compiler_params=_params(1), name=name,
    )(x, g.reshape(1, D_MODEL))


def _chunk_cols(w):
    w = w.reshape(w.shape[:-1] + (N_FF_CHUNKS, FF_CHUNK))
    return jnp.moveaxis(w, -2, 0)


def _unchunk_cols(w):
    w = jnp.moveaxis(w, 0, -2)
    return w.reshape(w.shape[:-2] + (D_FF,))


def _positions_last(x):
    return jnp.moveaxis(x, -3, -1)


def _positions_first(x):
    return jnp.moveaxis(x, -1, -3)


def kernel(x_prompt, x_sample, cache_swa_k, cache_swa_v, cache_sb_k, cache_sb_v, state_conv, page_table,
           norm_mix, norm_ffn, norm_final, w_qkv_a, sinks_a, w_o_a, w_qkv_b, bias_b, w_o_b,
           w_in_ffn, conv_w, conv_b, w_down):
    batch, seq, _ = x_prompt.shape
    dec_batch, dec_seq, _ = x_sample.shape
    depth = norm_mix.shape[0]
    past_len = page_table.shape[1] * cache_sb_k.shape[2]
    assert dec_seq == DEC_SEQ and cache_swa_k.shape[2] == WINDOW and cache_sb_k.shape[2] == PAGE_SIZE
    xp = x_prompt.reshape(batch * seq, D_MODEL)
    xs = x_sample.reshape(dec_batch * dec_seq, D_MODEL)
    rope_p = _rope_tables(jnp.arange(seq))
    rope_s = _rope_tables(past_len + jnp.arange(min(TOKEN_TILE, dec_batch * dec_seq)) % dec_seq)[:2]
    u = _suffix_matrix()
    swa_kc, swa_vc = _positions_last(cache_swa_k), _positions_last(cache_swa_v)
    sb_kc, sb_vc = _positions_last(cache_sb_k), _positions_last(cache_sb_v)

    swa_kp, swa_vp, swa_ks, swa_vs = [], [], [], []
    sb_kp, sb_vp, sb_ks, sb_vs = [], [], [], []
    conv_p, conv_s = [], []
    for i in range(depth):
        j = i // 2
        swa = i % 2 == 0
        w_qkv = (w_qkv_a if swa else w_qkv_b)[j]
        w_o = (w_o_a if swa else w_o_b)[j].astype(BF16)
        wqx = _expand_q_weight(w_qkv[:, :D_MODEL])
        wkv = w_qkv[:, D_MODEL:].astype(BF16)
        wk_t, wv_t = wkv[:, :D_KV].T, wkv[:, D_KV:].T
        qp, kp, vp, *blocks = _qkv_prompt_call(xp, norm_mix[i], wqx, wk_t, wv_t, rope_p if swa else None,
                                               batch, seq, not swa, f"qkv_prompt_{i}")
        qs, ks, vs = _qkv_sample_call(xs, norm_mix[i], wqx, wkv, rope_s if swa else None, f"qkv_sample_{i}")
        if swa:
            op = _swa_prompt_call(sinks_a[j], qp, kp, vp)
            os_, kw, vw = _swa_sample_call(sinks_a[j], qs, ks, vs, swa_kc, swa_vc, j)
            swa_kp.append(_positions_first(kp[..., seq - WINDOW:]))
            swa_vp.append(_positions_first(vp[..., seq - WINDOW:]))
            swa_ks.append(_positions_first(kw))
            swa_vs.append(_positions_first(vw))
        else:
            op = _sb_prompt_call(bias_b[j], qp, blocks[0], blocks[1], u)
            os_ = _sb_sample_call(page_table, bias_b[j], qs, ks, vs, u, sb_kc, sb_vc, j)
            sb_kp.append(_positions_first(kp))
            sb_vp.append(_positions_first(vp))
            sb_ks.append(ks.reshape(dec_batch, dec_seq, N_KV_HEADS, HEAD_DIM))
            sb_vs.append(vs.reshape(dec_batch, dec_seq, N_KV_HEADS, HEAD_DIM))
        wgu = jnp.concatenate([_chunk_cols(w_in_ffn[i][:, :D_FF]), _chunk_cols(w_in_ffn[i][:, D_FF:])],
                              axis=-1).astype(BF16)
        wd = w_down[i].reshape(N_FF_CHUNKS, FF_CHUNK, D_MODEL).astype(BF16)
        cw = _chunk_cols(conv_w[i])
        cb = _chunk_cols(conv_b[i][None, :])
        xp, cp = _ffn_call(xp, op, w_o, norm_ffn[i], wgu, cw, cb, wd, None, seq, f"ffn_prompt_{i}")
        xs, cs = _ffn_call(xs, os_, w_o, norm_ffn[i], wgu, cw, cb, wd, _chunk_cols(state_conv[i]), None,
                           f"ffn_sample_{i}")
        conv_p.append(_unchunk_cols(cp)[:, SUBLANES - (CONV_W - 1):])
        conv_s.append(_unchunk_cols(cs))
    y_prompt = _norm_call(xp, norm_final, "norm_prompt").reshape(batch, seq, D_MODEL)
    y_sample = _norm_call(xs, norm_final, "norm_sample").reshape(dec_batch, dec_seq, D_MODEL)
    return (y_prompt, y_sample,
            jnp.stack(swa_kp), jnp.stack(swa_vp), jnp.stack(sb_kp), jnp.stack(sb_vp), jnp.stack(conv_p),
            jnp.stack(swa_ks), jnp.stack(swa_vs), jnp.stack(sb_ks), jnp.stack(sb_vs), jnp.stack(conv_s))
```

```python
import functools

import jax
import jax.numpy as jnp
from jax import lax
from jax.experimental import pallas as pl
from jax.experimental.pallas import tpu as pltpu

F32 = jnp.float32
BF16 = jnp.bfloat16

D_MODEL = 1024
HEAD_DIM = 64
N_HEADS = 16
N_KV_HEADS = 4
GROUP = N_HEADS // N_KV_HEADS
D_KV = N_KV_HEADS * HEAD_DIM
DEC_SEQ = 8
WINDOW = 128
PAGE_SIZE = 128
ROPE_THETA = 10000.0
D_FF = 2816
CONV_W = 3
RMS_EPS = 1e-6
ATTN_SCALE = HEAD_DIM ** -0.5

LANES = 128
SUBLANES = 8
D_QX = N_HEADS * LANES
FF_CHUNK = 256
N_FF_CHUNKS = D_FF // FF_CHUNK
TOKEN_TILE = 512
SB_BLOCK = 2 * LANES
PAGES_PER_STEP = 16
VMEM_LIMIT = 56 * 1024 * 1024

assert D_FF % FF_CHUNK == 0 and TOKEN_TILE % SB_BLOCK == 0
assert DEC_SEQ == SUBLANES and WINDOW == LANES and PAGE_SIZE == LANES and 2 * HEAD_DIM == LANES


def _params(n_axes):
    return pltpu.CompilerParams(dimension_semantics=("arbitrary",) * n_axes,
                                vmem_limit_bytes=VMEM_LIMIT)


def _resident(shape, index_map):
    return pl.BlockSpec(shape, index_map, pipeline_mode=pl.Buffered(1))


def _rmsnorm(x, g):
    r = lax.rsqrt(jnp.mean(x * x, axis=-1, keepdims=True) + RMS_EPS)
    return x * r * g


def _dot(a, b):
    return jnp.dot(a, b, preferred_element_type=F32)


def _dot_nt(a, b):
    return lax.dot_general(a, b, (((1,), (1,)), ((), ())), preferred_element_type=F32)


def _kv_pair(kv):
    return kv // 2, kv % 2


def _rope_tile(x, cosf, sinf):
    lane = lax.broadcasted_iota(jnp.int32, x.shape, 1)
    first = (lane % HEAD_DIM) < (HEAD_DIM // 2)
    other = jnp.where(first, pltpu.roll(x, LANES - HEAD_DIM // 2, axis=1),
                      pltpu.roll(x, HEAD_DIM // 2, axis=1))
    return x * cosf + other * sinf


def _rope_rows(x, cos_t, sin_t):
    half = HEAD_DIM // 2
    out = []
    for kv in range(N_KV_HEADS):
        x1 = x[kv * HEAD_DIM:kv * HEAD_DIM + half]
        x2 = x[kv * HEAD_DIM + half:(kv + 1) * HEAD_DIM]
        out += [x1 * cos_t - x2 * sin_t, x2 * cos_t + x1 * sin_t]
    return jnp.concatenate(out, axis=0)


def _qkv_prompt_body(*refs, rope, blocked):
    refs = list(refs)
    x_ref, g_ref, wq_ref, wk_ref, wv_ref = refs[:5]
    refs = refs[5:]
    if rope:
        cos_ref, sin_ref, cos_t_ref, sin_t_ref = refs[:4]
        refs = refs[4:]
    q_ref, k_ref, v_ref = refs[:3]
    h = _rmsnorm(x_ref[...], g_ref[...]).astype(BF16)
    tm = h.shape[0]
    q = _dot(h, wq_ref[...])
    k_t = _dot_nt(wk_ref[...], h)
    v_t = _dot_nt(wv_ref[...], h)
    if rope:
        cosf, sinf = cos_ref[...], sin_ref[...]
        for j in range(N_HEADS):
            sl = slice(j * LANES, (j + 1) * LANES)
            q_ref[:, sl] = _rope_tile(q[:, sl], cosf, sinf).astype(q_ref.dtype)
        k_t = _rope_rows(k_t, cos_t_ref[...], sin_t_ref[...])
    else:
        q_ref[...] = q.astype(q_ref.dtype)
    k_ref[...] = k_t.reshape(N_KV_HEADS, HEAD_DIM, tm)
    v_ref[...] = v_t.reshape(N_KV_HEADS, HEAD_DIM, tm)
    if blocked:
        kb_ref, vb_ref = refs[3:5]
        for c in range(tm // SB_BLOCK):
            sl = slice(c * SB_BLOCK, (c + 1) * SB_BLOCK)
            kb_ref[c] = k_t[:, sl].reshape(N_KV_HEADS, HEAD_DIM, SB_BLOCK)
            vb_ref[c] = v_t[:, sl].reshape(N_KV_HEADS, HEAD_DIM, SB_BLOCK)


def _qkv_prompt_call(x, g, wqx, wk_t, wv_t, tables, batch, seq, blocked, name):
    tm = TOKEN_TILE
    tps = seq // tm
    rope = tables is not None
    row = lambda i: (i, 0)
    const = lambda i: (0, 0)
    in_specs = [pl.BlockSpec((tm, D_MODEL), row),
                pl.BlockSpec((1, D_MODEL), const),
                _resident((D_MODEL, D_QX), const),
                _resident((D_KV, D_MODEL), const),
                _resident((D_KV, D_MODEL), const)]
    args = [x, g.reshape(1, D_MODEL), wqx, wk_t, wv_t]
    if rope:
        cosf, sinf, cos_t, sin_t = tables
        in_specs += [pl.BlockSpec((tm, LANES), lambda i: (i % tps, 0))] * 2
        in_specs += [pl.BlockSpec((HEAD_DIM // 2, tm), lambda i: (0, i % tps))] * 2
        args += [cosf, sinf, cos_t, sin_t]
    kv_spec = pl.BlockSpec((None, N_KV_HEADS, HEAD_DIM, tm), lambda i: (i // tps, 0, 0, i % tps))
    kv_shape = jax.ShapeDtypeStruct((batch, N_KV_HEADS, HEAD_DIM, seq), F32)
    out_specs = [pl.BlockSpec((tm, D_QX), row), kv_spec, kv_spec]
    out_shape = [jax.ShapeDtypeStruct((batch * seq, D_QX), BF16), kv_shape, kv_shape]
    if blocked:
        nb = tm // SB_BLOCK
        b_spec = pl.BlockSpec((None, nb, N_KV_HEADS, HEAD_DIM, SB_BLOCK), lambda i: (i // tps, i % tps, 0, 0, 0))
        b_shape = jax.ShapeDtypeStruct((batch, seq // SB_BLOCK, N_KV_HEADS, HEAD_DIM, SB_BLOCK), F32)
        out_specs += [b_spec, b_spec]
        out_shape += [b_shape, b_shape]
    return pl.pallas_call(
        functools.partial(_qkv_prompt_body, rope=rope, blocked=blocked),
        grid=(batch * seq // tm,),
        in_specs=in_specs, out_specs=out_specs, out_shape=out_shape,
        compiler_params=_params(1), name=name,
    )(*args)


def _qkv_sample_body(*refs, rope):
    if rope:
        x_ref, g_ref, wq_ref, wkv_ref, cos_ref, sin_ref, q_ref, k_ref, v_ref = refs
    else:
        x_ref, g_ref, wq_ref, wkv_ref, q_ref, k_ref, v_ref = refs
    h = _rmsnorm(x_ref[...], g_ref[...]).astype(BF16)
    q = _dot(h, wq_ref[...])
    kv = _dot(h, wkv_ref[...])
    if rope:
        cosf, sinf = cos_ref[...], sin_ref[...]
        for j in range(N_HEADS):
            sl = slice(j * LANES, (j + 1) * LANES)
            q_ref[:, sl] = _rope_tile(q[:, sl], cosf, sinf)
        for j in range(D_KV // LANES):
            sl = slice(j * LANES, (j + 1) * LANES)
            k_ref[:, sl] = _rope_tile(kv[:, sl], cosf, sinf)
    else:
        q_ref[...] = q
        k_ref[...] = kv[:, :D_KV]
    v_ref[...] = kv[:, D_KV:]


def _qkv_sample_call(x, g, wqx, wkv, tables, name):
    t = x.shape[0]
    tm = min(TOKEN_TILE, t)
    rope = tables is not None
    row = lambda i: (i, 0)
    const = lambda i: (0, 0)
    in_specs = [pl.BlockSpec((tm, D_MODEL), row),
                pl.BlockSpec((1, D_MODEL), const),
                _resident((D_MODEL, D_QX), const),
                _resident((D_MODEL, 2 * D_KV), const)]
    args = [x, g.reshape(1, D_MODEL), wqx, wkv]
    if rope:
        in_specs += [pl.BlockSpec((tm, LANES), const)] * 2
        args += list(tables)
    return pl.pallas_call(
        functools.partial(_qkv_sample_body, rope=rope),
        grid=(t // tm,),
        in_specs=in_specs,
        out_specs=[pl.BlockSpec((tm, D_QX), row), pl.BlockSpec((tm, D_KV), row), pl.BlockSpec((tm, D_KV), row)],
        out_shape=[jax.ShapeDtypeStruct((t, D_QX), F32),
                   jax.ShapeDtypeStruct((t, D_KV), F32),
                   jax.ShapeDtypeStruct((t, D_KV), F32)],
        compiler_params=_params(1), name=name,
    )(*args)


def _rope_tables(pos):
    inv_freq = ROPE_THETA ** (-jnp.arange(0, HEAD_DIM, 2, dtype=F32) / HEAD_DIM)
    ang = pos.astype(F32)[:, None] * inv_freq[None, :]
    cos, sin = jnp.cos(ang), jnp.sin(ang)
    cosf = jnp.concatenate([cos, cos, cos, cos], axis=1)
    sinf = jnp.concatenate([-sin, sin, -sin, sin], axis=1)
    return cosf, sinf, cos.T, sin.T


def _expand_q_weight(w_q):
    w = (w_q * ATTN_SCALE).reshape(D_MODEL, N_HEADS, 1, HEAD_DIM)
    half = ((jnp.arange(N_HEADS) // GROUP) % 2)[None, :, None, None]
    w = jnp.where(half == jnp.arange(2)[None, None, :, None], w, 0.0)
    return w.reshape(D_MODEL, D_QX).astype(BF16)


def _half_mask(shape, half):
    lane = lax.broadcasted_iota(jnp.int32, shape, 1)
    return (lane >= HEAD_DIM) if half else (lane < HEAD_DIM)


def _merge_heads(even, odd, even_half, odd_half):
    if even_half != 0:
        even = pltpu.roll(even, HEAD_DIM, axis=1)
    if odd_half != 1:
        odd = pltpu.roll(odd, HEAD_DIM, axis=1)
    return jnp.where(_half_mask(even.shape, 0), even, odd)


def _merge_all_heads(parts):
    tiles = [_merge_heads(parts[2 * p][0], parts[2 * p + 1][0], parts[2 * p][1], parts[2 * p + 1][1])
             for p in range(N_HEADS // 2)]
    return jnp.concatenate(tiles, axis=1)


def _block_diag_q(qx):
    zeros = jnp.zeros((SUBLANES, LANES), qx.dtype)
    rows = []
    for h in range(N_HEADS):
        pair, _ = _kv_pair(h // GROUP)
        tile = qx[:, h * LANES:(h + 1) * LANES]
        rows.append(jnp.concatenate([tile, zeros] if pair == 0 else [zeros, tile], axis=1))
    return jnp.concatenate(rows, axis=0)


def _unpack_block_diag(acc):
    parts = []
    for h in range(N_HEADS):
        pair, half = _kv_pair(h // GROUP)
        parts.append((acc[h * SUBLANES:(h + 1) * SUBLANES, pair * LANES:(pair + 1) * LANES], half))
    return _merge_all_heads(parts)


def _per_row_heads(scalar_ref, rows_per_head, width):
    return jnp.concatenate([jnp.full((rows_per_head, width), scalar_ref[h], F32) for h in range(N_HEADS)], axis=0)


def _pad_rows(pad_scr, x):
    pad_scr[...] = jnp.zeros(pad_scr.shape, F32)
    pad_scr[0:DEC_SEQ, :] = x
    return pad_scr[...]


def _swa_prompt_body(sink_ref, q_ref, kp_ref, kc_ref, vp_ref, vc_ref, o_ref):
    j = pl.program_id(1)
    rows = GROUP * WINDOW
    a = lax.broadcasted_iota(jnp.int32, (rows, 2 * WINDOW), 0) % WINDOW
    c = lax.broadcasted_iota(jnp.int32, (rows, 2 * WINDOW), 1)
    mask = (c >= a) & (c <= a + WINDOW) & ((j > 0) | (c >= WINDOW))
    parts = []
    for kv in range(N_KV_HEADS):
        pair, half = _kv_pair(kv)
        band = lambda prev, cur: jnp.concatenate(
            [prev[2 * pair:2 * pair + 2].reshape(LANES, WINDOW), cur[2 * pair:2 * pair + 2].reshape(LANES, WINDOW)],
            axis=1).astype(BF16)
        k_t = band(kp_ref, kc_ref)
        v_t = band(vp_ref, vc_ref)
        heads = range(kv * GROUP, (kv + 1) * GROUP)
        q4 = jnp.concatenate([q_ref[:, h * LANES:(h + 1) * LANES] for h in heads], axis=0)
        sink = jnp.concatenate([jnp.full((WINDOW, 1), sink_ref[h], F32) for h in heads], axis=0)
        z = jnp.where(mask, _dot(q4, k_t), -jnp.inf)
        m = jnp.maximum(jnp.max(z, axis=-1, keepdims=True), sink)
        p = jnp.exp(z - m)
        denom = jnp.sum(p, axis=-1, keepdims=True) + jnp.exp(sink - m)
        o4 = _dot_nt(p.astype(BF16), v_t) / denom
        parts += [(o4[g * WINDOW:(g + 1) * WINDOW], half) for g in range(GROUP)]
    o_ref[...] = _merge_all_heads(parts).astype(o_ref.dtype)


def _swa_prompt_call(sink, qx, k_t, v_t):
    batch, _, _, seq = k_t.shape
    nb = seq // WINDOW
    kv_spec = lambda f: pl.BlockSpec((None, N_KV_HEADS, HEAD_DIM, WINDOW), f)
    cur = lambda b, j: (b, 0, 0, j)
    prev = lambda b, j: (b, 0, 0, jnp.maximum(j - 1, 0))
    return pl.pallas_call(
        _swa_prompt_body,
        grid=(batch, nb),
        in_specs=[pl.BlockSpec(memory_space=pltpu.SMEM),
                  pl.BlockSpec((WINDOW, D_QX), lambda b, j: (b * nb + j, 0)),
                  kv_spec(prev), kv_spec(cur), kv_spec(prev), kv_spec(cur)],
        out_specs=pl.BlockSpec((WINDOW, D_MODEL), lambda b, j: (b * nb + j, 0)),
        out_shape=jax.ShapeDtypeStruct((batch * seq, D_MODEL), BF16),
        compiler_params=_params(2), name="swa_prompt",
    )(sink, qx, k_t, k_t, v_t, v_t)


def _swa_sample_body(sink_ref, q_ref, kn_ref, vn_ref, kc_ref, vc_ref, o_ref, kw_ref, vw_ref, pad_scr):
    rows = N_HEADS * DEC_SEQ
    t = lax.broadcasted_iota(jnp.int32, (rows, WINDOW), 0) % DEC_SEQ
    c = lax.broadcasted_iota(jnp.int32, (rows, WINDOW), 1)
    kc_t = kc_ref[...].reshape(D_KV, WINDOW)
    vc_t = vc_ref[...].reshape(D_KV, WINDOW)
    kn = _pad_rows(pad_scr, kn_ref[...])
    vn = _pad_rows(pad_scr, vn_ref[...])

    lane = lax.broadcasted_iota(jnp.int32, (D_KV, WINDOW), 1)
    keep = WINDOW - DEC_SEQ
    shift = lambda x: pltpu.roll(x, keep, axis=1)
    kw_ref[...] = jnp.where(lane < keep, shift(kc_t), shift(kn.T)).reshape(N_KV_HEADS, HEAD_DIM, WINDOW)
    vw_ref[...] = jnp.where(lane < keep, shift(vc_t), shift(vn.T)).reshape(N_KV_HEADS, HEAD_DIM, WINDOW)

    qbd = _block_diag_q(q_ref[...]).astype(BF16)
    sink = _per_row_heads(sink_ref, DEC_SEQ, 1)
    zc = jnp.where(c >= t, _dot(qbd, kc_t.astype(BF16)), -jnp.inf)
    zn = jnp.where(c <= t, _dot_nt(qbd, kn.astype(BF16)), -jnp.inf)
    m = jnp.maximum(jnp.maximum(jnp.max(zc, axis=-1, keepdims=True), jnp.max(zn, axis=-1, keepdims=True)), sink)
    pc = jnp.exp(zc - m)
    pn = jnp.exp(zn - m)
    denom = jnp.sum(pc, axis=-1, keepdims=True) + jnp.sum(pn, axis=-1, keepdims=True) + jnp.exp(sink - m)
    acc = (_dot_nt(pc.astype(BF16), vc_t.astype(BF16)) + _dot(pn.astype(BF16), vn.astype(BF16))) / denom
    o_ref[...] = _unpack_block_diag(acc)


def _swa_sample_call(sink, qx, k_new, v_new, k_cache_t, v_cache_t, layer):
    n = k_cache_t.shape[1]
    seq_rows = lambda s: (s, 0)
    cache_in = pl.BlockSpec((None, None, N_KV_HEADS, HEAD_DIM, WINDOW), lambda s: (layer, s, 0, 0, 0))
    cache_out = pl.BlockSpec((None, N_KV_HEADS, HEAD_DIM, WINDOW), lambda s: (s, 0, 0, 0))
    win_shape = jax.ShapeDtypeStruct((n, N_KV_HEADS, HEAD_DIM, WINDOW), F32)
    return pl.pallas_call(
        _swa_sample_body,
        grid=(n,),
        in_specs=[pl.BlockSpec(memory_space=pltpu.SMEM),
                  pl.BlockSpec((DEC_SEQ, D_QX), seq_rows),
                  pl.BlockSpec((DEC_SEQ, D_KV), seq_rows),
                  pl.BlockSpec((DEC_SEQ, D_KV), seq_rows),
                  cache_in, cache_in],
        out_specs=[pl.BlockSpec((DEC_SEQ, D_MODEL), seq_rows), cache_out, cache_out],
        out_shape=[jax.ShapeDtypeStruct((n * DEC_SEQ, D_MODEL), F32), win_shape, win_shape],
        scratch_shapes=[pltpu.VMEM((WINDOW, D_KV), F32)],
        compiler_params=_params(1), name="swa_sample",
    )(sink, qx, k_new, v_new, k_cache_t, v_cache_t)


def _suffix_matrix():
    j = jnp.arange(2 * LANES)[:, None] % LANES
    s = jnp.arange(2 * LANES)[None, :]
    return ((s >= LANES) | (j > s)).astype(BF16)


def _softplus(z):
    return jnp.maximum(z, 0.0) + jnp.log(1.0 + jnp.exp(-jnp.abs(z)))


def _hi_lo(x):
    hi = x.astype(BF16)
    lo = (x - hi.astype(F32)).astype(BF16)
    return jnp.concatenate([hi, lo], axis=1)


def _sb_blocks(z_blocks, masks, u, carry):
    m = z_blocks[0].shape[0]
    sps = [_softplus(z) for z in z_blocks]
    log_rest = [-sp if mk is None else jnp.where(mk, -sp, 0.0) for sp, mk in zip(sps, masks)]
    sums = _dot(jnp.concatenate([_hi_lo(x) for x in log_rest], axis=0), u)
    weights = []
    for n, (z, sp, mk) in enumerate(zip(z_blocks, sps, masks)):
        s = sums[n * m:(n + 1) * m]
        a = jnp.exp(z - sp + s[:, :LANES] + carry)
        weights.append(a if mk is None else jnp.where(mk, a, 0.0))
        carry = carry + s[:, LANES:]
    return weights, carry


def _sb_prompt_body(bias_ref, q_ref, k_ref, v_ref, u_ref, o_ref, carry_scr, acc_scr):
    i = pl.program_id(1)
    rows = GROUP * SB_BLOCK
    sub = SB_BLOCK // LANES
    u = u_ref[...]
    t = lax.broadcasted_iota(jnp.int32, (rows, LANES), 0) % SB_BLOCK
    s = lax.broadcasted_iota(jnp.int32, (rows, LANES), 1)
    diag_masks = [(s + n * LANES) < t for n in reversed(range(sub))]

    def block(c, masks, first):
        for kv in range(N_KV_HEADS):
            pair, _ = _kv_pair(kv)
            heads = range(kv * GROUP, (kv + 1) * GROUP)
            q4 = jnp.concatenate([q_ref[:, h * LANES:(h + 1) * LANES] for h in heads], axis=0)
            bias = jnp.concatenate([jnp.full((SB_BLOCK, LANES), bias_ref[h], F32) for h in heads], axis=0)
            k_t = k_ref[c, 2 * pair:2 * pair + 2].reshape(LANES, SB_BLOCK).astype(BF16)
            v_t = v_ref[c, 2 * pair:2 * pair + 2].reshape(LANES, SB_BLOCK).astype(BF16)
            z = _dot(q4, k_t)
            zs = [z[:, n * LANES:(n + 1) * LANES] + bias for n in reversed(range(sub))]
            carry = jnp.zeros((rows, LANES), F32) if first else carry_scr[kv]
            weights, carry = _sb_blocks(zs, masks, u, carry)
            a = jnp.concatenate(list(reversed(weights)), axis=1).astype(BF16)
            pv = _dot_nt(a, v_t)
            acc_scr[kv] = pv if first else acc_scr[kv] + pv
            carry_scr[kv] = carry

    block(i, diag_masks, True)
    lax.fori_loop(0, i, lambda n, _: (block(i - 1 - n, [None] * sub, False), 0)[1], 0)
    parts = []
    for kv in range(N_KV_HEADS):
        acc = acc_scr[kv]
        parts += [(acc[g * SB_BLOCK:(g + 1) * SB_BLOCK], _kv_pair(kv)[1]) for g in range(GROUP)]
    o_ref[...] = _merge_all_heads(parts).astype(o_ref.dtype)


def _sb_prompt_call(bias, qx, k_blocks, v_blocks, u):
    batch, nb = k_blocks.shape[:2]
    rows = GROUP * SB_BLOCK
    kv_spec = pl.BlockSpec((None, nb, N_KV_HEADS, HEAD_DIM, SB_BLOCK), lambda b, i: (b, 0, 0, 0, 0))
    return pl.pallas_call(
        _sb_prompt_body,
        grid=(batch, nb),
        in_specs=[pl.BlockSpec(memory_space=pltpu.SMEM),
                  pl.BlockSpec((SB_BLOCK, D_QX), lambda b, i: (b * nb + i, 0)),
                  kv_spec, kv_spec,
                  pl.BlockSpec((2 * LANES, 2 * LANES), lambda b, i: (0, 0))],
        out_specs=pl.BlockSpec((SB_BLOCK, D_MODEL), lambda b, i: (b * nb + i, 0)),
        out_shape=jax.ShapeDtypeStruct((batch * nb * SB_BLOCK, D_MODEL), BF16),
        scratch_shapes=[pltpu.VMEM((N_KV_HEADS, rows, LANES), F32),
                        pltpu.VMEM((N_KV_HEADS, rows, LANES), F32)],
        compiler_params=_params(2), name="sb_prompt",
    )(bias, qx, k_blocks, v_blocks, u)


def _page_copies(pt_ref, pool_k, pool_v, kbuf, vbuf, sems, layer, seq, chunk, slot):
    n_pages = pt_ref.shape[1]
    copies = []
    for j in range(PAGES_PER_STEP):
        page = pt_ref[seq, n_pages - 1 - (chunk * PAGES_PER_STEP + j)]
        copies.append(pltpu.make_async_copy(pool_k.at[layer, page], kbuf.at[slot, j], sems.at[slot, 0]))
        copies.append(pltpu.make_async_copy(pool_v.at[layer, page], vbuf.at[slot, j], sems.at[slot, 1]))
    return copies


def _sb_sample_body(pt_ref, bias_ref, q_ref, kn_ref, vn_ref, u_ref, pool_k, pool_v, o_ref,
                    qbd_scr, bias_scr, carry_scr, acc_scr, pad_scr, kbuf, vbuf, sems, *, layer):
    s, c = pl.program_id(0), pl.program_id(1)
    n_seq, n_chunks = pl.num_programs(0), pl.num_programs(1)
    step = s * n_chunks + c
    slot = step % 2
    rows = N_HEADS * DEC_SEQ
    u = u_ref[...]
    copies = functools.partial(_page_copies, pt_ref, pool_k, pool_v, kbuf, vbuf, sems, layer)

    @pl.when(step == 0)
    def _first_fetch():
        for cp in copies(s, c, slot):
            cp.start()

    @pl.when(c == 0)
    def _new_tokens():
        qbd = _block_diag_q(q_ref[...]).astype(BF16)
        bias = _per_row_heads(bias_ref, DEC_SEQ, LANES)
        qbd_scr[...] = qbd
        bias_scr[...] = bias
        kn = _pad_rows(pad_scr, kn_ref[...]).astype(BF16)
        vn = _pad_rows(pad_scr, vn_ref[...]).astype(BF16)
        t = lax.broadcasted_iota(jnp.int32, (rows, LANES), 0) % DEC_SEQ
        s = lax.broadcasted_iota(jnp.int32, (rows, LANES), 1)
        (a,), carry = _sb_blocks([_dot_nt(qbd, kn) + bias], [s < t], u, jnp.zeros((rows, LANES), F32))
        carry_scr[...] = carry
        acc_scr[...] = _dot(a.astype(BF16), vn)

    last = step == n_seq * n_chunks - 1
    wrap = c == n_chunks - 1
    nxt_s = jnp.where(last, s, jnp.where(wrap, s + 1, s))
    nxt_c = jnp.where(last, c, jnp.where(wrap, 0, c + 1))
    for cp in copies(s, c, slot):
        cp.wait()
    for cp in copies(nxt_s, nxt_c, 1 - slot):
        cp.start()

    qbd, bias = qbd_scr[...], bias_scr[...]
    page = lambda buf, j: buf[slot, j].reshape(D_KV, PAGE_SIZE).astype(BF16)
    k_t = jnp.concatenate([page(kbuf, j) for j in range(PAGES_PER_STEP)], axis=1)
    v_t = jnp.concatenate([page(vbuf, j) for j in range(PAGES_PER_STEP)], axis=1)
    z = _dot(qbd, k_t)
    zs = [z[:, n * LANES:(n + 1) * LANES] + bias for n in range(PAGES_PER_STEP)]
    weights, carry = _sb_blocks(zs, [None] * PAGES_PER_STEP, u, carry_scr[...])
    acc = acc_scr[...] + _dot_nt(jnp.concatenate(weights, axis=1).astype(BF16), v_t)
    carry_scr[...] = carry
    acc_scr[...] = acc

    @pl.when(wrap)
    def _sequence_done():
        o_ref[...] = _unpack_block_diag(acc)

    @pl.when(last)
    def _drain():
        for cp in copies(nxt_s, nxt_c, 1 - slot):
            cp.wait()


def _sb_sample_call(page_table, bias, qx, k_new, v_new, u, pool_k_t, pool_v_t, layer):
    n, n_pages = page_table.shape
    assert n_pages % PAGES_PER_STEP == 0
    seq_rows = lambda s, c, pt: (s, 0)
    page_buf = pltpu.VMEM((2, PAGES_PER_STEP, N_KV_HEADS, HEAD_DIM, PAGE_SIZE), F32)
    rows = N_HEADS * DEC_SEQ
    grid_spec = pltpu.PrefetchScalarGridSpec(
        num_scalar_prefetch=1,
        grid=(n, n_pages // PAGES_PER_STEP),
        in_specs=[pl.BlockSpec(memory_space=pltpu.SMEM),
                  pl.BlockSpec((DEC_SEQ, D_QX), seq_rows),
                  pl.BlockSpec((DEC_SEQ, D_KV), seq_rows),
                  pl.BlockSpec((DEC_SEQ, D_KV), seq_rows),
                  pl.BlockSpec((2 * LANES, 2 * LANES), lambda s, c, pt: (0, 0)),
                  pl.BlockSpec(memory_space=pl.ANY),
                  pl.BlockSpec(memory_space=pl.ANY)],
        out_specs=pl.BlockSpec((DEC_SEQ, D_MODEL), seq_rows),
        scratch_shapes=[pltpu.VMEM((rows, D_KV), BF16),
                        pltpu.VMEM((rows, LANES), F32),
                        pltpu.VMEM((rows, LANES), F32),
                        pltpu.VMEM((rows, D_KV), F32),
                        pltpu.VMEM((PAGE_SIZE, D_KV), F32),
                        page_buf, page_buf,
                        pltpu.SemaphoreType.DMA((2, 2))],
    )
    return pl.pallas_call(
        functools.partial(_sb_sample_body, layer=layer),
        grid_spec=grid_spec,
        out_shape=jax.ShapeDtypeStruct((n * DEC_SEQ, D_MODEL), F32),
        compiler_params=_params(2), name="sb_sample",
    )(page_table, bias, qx, k_new, v_new, u, pool_k_t, pool_v_t)


def _ffn_body(*refs, sample, tiles_per_seq, final):
    refs = list(refs)
    x_ref, o_ref, wo_ref, g_ref, win_ref, cw_ref, cb_ref, wd_ref = refs[:8]
    refs = refs[8:]
    prev_ref = refs.pop(0) if sample else None
    gf_ref = refs.pop(0) if final else None
    xo_ref, conv_ref, h_scr, acc_scr = refs[:4]
    carry_scr = None if sample else refs[4]
    i = pl.program_id(0)
    tm = x_ref.shape[0]
    x1 = x_ref[...] + _dot(o_ref[...].astype(BF16), wo_ref[...])
    h_scr[...] = _rmsnorm(x1, g_ref[...]).astype(BF16)
    acc_scr[...] = x1
    if not sample:
        @pl.when(i % tiles_per_seq == 0)
        def _sequence_start():
            carry_scr[...] = jnp.zeros(carry_scr.shape, F32)

    def activation(c):
        sl = slice(c * FF_CHUNK, (c + 1) * FF_CHUNK)
        h = h_scr[...]
        gate = _dot(h, win_ref[:, sl])
        up = _dot(h, win_ref[:, D_FF + c * FF_CHUNK:D_FF + (c + 1) * FF_CHUNK])
        cw = cw_ref[:, sl]
        if sample:
            nseq = tm // DEC_SEQ
            prev = prev_ref[:, :, sl]
            expand = lambda p: jnp.broadcast_to(p, (nseq, DEC_SEQ, FF_CHUNK)).reshape(tm, FF_CHUNK)
            p0, p1 = expand(prev[:, 0:1, :]), expand(prev[:, 1:2, :])
            t = lax.broadcasted_iota(jnp.int32, (tm, FF_CHUNK), 0) % DEC_SEQ
            g1 = jnp.where(t == 0, p1, pltpu.roll(gate, 1, axis=0))
            g2 = jnp.where(t == 0, p0, jnp.where(t == 1, p1, pltpu.roll(gate, 2, axis=0)))
            conv_ref[:, :, sl] = gate.reshape(nseq, DEC_SEQ, FF_CHUNK)[:, DEC_SEQ - (CONV_W - 1):, :]
        else:
            prev8 = carry_scr[:, sl]
            top = gate[0:SUBLANES]
            r8 = lax.broadcasted_iota(jnp.int32, (SUBLANES, FF_CHUNK), 0)
            top1 = jnp.where(r8 < 1, pltpu.roll(prev8, 1, axis=0), pltpu.roll(top, 1, axis=0))
            top2 = jnp.where(r8 < 2, pltpu.roll(prev8, 2, axis=0), pltpu.roll(top, 2, axis=0))
            g1 = jnp.concatenate([top1, pltpu.roll(gate, 1, axis=0)[SUBLANES:]], axis=0)
            g2 = jnp.concatenate([top2, pltpu.roll(gate, 2, axis=0)[SUBLANES:]], axis=0)
            last8 = gate[tm - SUBLANES:]
            carry_scr[:, sl] = last8
            conv_ref[:, sl] = last8
        gc = cb_ref[:, sl] + cw[0:1] * g2 + cw[1:2] * g1 + cw[2:3] * gate
        return (gc / (1.0 + jnp.exp(-gc)) * up).astype(BF16)

    act = activation(0)
    for c in range(1, N_FF_CHUNKS):
        nxt = activation(c)
        acc_scr[...] += _dot(act, wd_ref[(c - 1) * FF_CHUNK:c * FF_CHUNK, :])
        act = nxt
    x2 = acc_scr[...] + _dot(act, wd_ref[(N_FF_CHUNKS - 1) * FF_CHUNK:, :])
    xo_ref[...] = _rmsnorm(x2, gf_ref[...]) if final else x2


def _ffn_call(x, o, wo, g, w_in, cw, cb, wd, prev, g_final, seq, name):
    t = x.shape[0]
    sample = prev is not None
    final = g_final is not None
    tm = min(TOKEN_TILE, t)
    tiles_per_seq = None if sample else seq // tm
    row = lambda i: (i, 0)
    const = lambda i: (0, 0)
    in_specs = [pl.BlockSpec((tm, D_MODEL), row),
                pl.BlockSpec((tm, D_MODEL), row),
                _resident((D_MODEL, D_MODEL), const),
                pl.BlockSpec((1, D_MODEL), const),
                _resident((D_MODEL, 2 * D_FF), const),
                pl.BlockSpec((CONV_W, D_FF), const),
                pl.BlockSpec((1, D_FF), const),
                _resident((D_FF, D_MODEL), const)]
    args = [x, o, wo, g.reshape(1, D_MODEL), w_in, cw, cb.reshape(1, D_FF), wd]
    scratch = [pltpu.VMEM((tm, D_MODEL), BF16), pltpu.VMEM((tm, D_MODEL), F32)]
    if sample:
        nseq = tm // DEC_SEQ
        state_spec = pl.BlockSpec((nseq, CONV_W - 1, D_FF), lambda i: (i, 0, 0))
        in_specs.append(state_spec)
        args.append(prev)
        conv_spec = state_spec
        conv_shape = jax.ShapeDtypeStruct((t // DEC_SEQ, CONV_W - 1, D_FF), F32)
    else:
        conv_spec = pl.BlockSpec((None, SUBLANES, D_FF), lambda i: (i // tiles_per_seq, 0, 0))
        conv_shape = jax.ShapeDtypeStruct((t // seq, SUBLANES, D_FF), F32)
        scratch.append(pltpu.VMEM((SUBLANES, D_FF), F32))
    if final:
        in_specs.append(pl.BlockSpec((1, D_MODEL), const))
        args.append(g_final.reshape(1, D_MODEL))
    return pl.pallas_call(
        functools.partial(_ffn_body, sample=sample, tiles_per_seq=tiles_per_seq, final=final),
        grid=(t // tm,),
        in_specs=in_specs,
        out_specs=[pl.BlockSpec((tm, D_MODEL), row), conv_spec],
        out_shape=[jax.ShapeDtypeStruct((t, D_MODEL), F32), conv_shape],
        scratch_shapes=scratch,
        compiler_params=_params(1), name=name,
    )(*args)


def _positions_last(x):
    return jnp.moveaxis(x, -3, -1)


def _positions_first(x):
    return jnp.moveaxis(x, -1, -3)


def kernel(x_prompt, x_sample, cache_swa_k, cache_swa_v, cache_sb_k, cache_sb_v, state_conv, page_table,
           norm_mix, norm_ffn, norm_final, w_qkv_a, sinks_a, w_o_a, w_qkv_b, bias_b, w_o_b,
           w_in_ffn, conv_w, conv_b, w_down):
    batch, seq, _ = x_prompt.shape
    dec_batch, dec_seq, _ = x_sample.shape
    depth = norm_mix.shape[0]
    past_len = page_table.shape[1] * cache_sb_k.shape[2]
    assert dec_seq == DEC_SEQ and cache_swa_k.shape[2] == WINDOW and cache_sb_k.shape[2] == PAGE_SIZE
    xp = x_prompt.reshape(batch * seq, D_MODEL)
    xs = x_sample.reshape(dec_batch * dec_seq, D_MODEL)
    rope_p = _rope_tables(jnp.arange(seq))
    rope_s = _rope_tables(past_len + jnp.arange(min(TOKEN_TILE, dec_batch * dec_seq)) % dec_seq)[:2]
    u = _suffix_matrix()
    swa_kc, swa_vc = _positions_last(cache_swa_k), _positions_last(cache_swa_v)
    sb_kc, sb_vc = _positions_last(cache_sb_k), _positions_last(cache_sb_v)

    swa_kp, swa_vp, swa_ks, swa_vs = [], [], [], []
    sb_kp, sb_vp, sb_ks, sb_vs = [], [], [], []
    conv_p, conv_s = [], []
    for i in range(depth):
        j = i // 2
        swa = i % 2 == 0
        w_qkv = (w_qkv_a if swa else w_qkv_b)[j]
        w_o = (w_o_a if swa else w_o_b)[j].astype(BF16)
        wqx = _expand_q_weight(w_qkv[:, :D_MODEL])
        wkv = w_qkv[:, D_MODEL:].astype(BF16)
        wk_t, wv_t = wkv[:, :D_KV].T, wkv[:, D_KV:].T
        qp, kp, vp, *blocks = _qkv_prompt_call(xp, norm_mix[i], wqx, wk_t, wv_t, rope_p if swa else None,
                                               batch, seq, not swa, f"qkv_prompt_{i}")
        qs, ks, vs = _qkv_sample_call(xs, norm_mix[i], wqx, wkv, rope_s if swa else None, f"qkv_sample_{i}")
        if swa:
            op = _swa_prompt_call(sinks_a[j], qp, kp, vp)
            os_, kw, vw = _swa_sample_call(sinks_a[j], qs, ks, vs, swa_kc, swa_vc, j)
            swa_kp.append(_positions_first(kp[..., seq - WINDOW:]))
            swa_vp.append(_positions_first(vp[..., seq - WINDOW:]))
            swa_ks.append(_positions_first(kw))
            swa_vs.append(_positions_first(vw))
        else:
            op = _sb_prompt_call(bias_b[j], qp, blocks[0], blocks[1], u)
            os_ = _sb_sample_call(page_table, bias_b[j], qs, ks, vs, u, sb_kc, sb_vc, j)
            sb_kp.append(_positions_first(kp))
            sb_vp.append(_positions_first(vp))
            sb_ks.append(ks.reshape(dec_batch, dec_seq, N_KV_HEADS, HEAD_DIM))
            sb_vs.append(vs.reshape(dec_batch, dec_seq, N_KV_HEADS, HEAD_DIM))
        g_final = norm_final if i == depth - 1 else None
        ffn_w = (w_o, norm_ffn[i], w_in_ffn[i].astype(BF16), conv_w[i], conv_b[i], w_down[i].astype(BF16))
        xp, cp = _ffn_call(xp, op, *ffn_w, None, g_final, seq, f"ffn_prompt_{i}")
        xs, cs = _ffn_call(xs, os_, *ffn_w, state_conv[i], g_final, None, f"ffn_sample_{i}")
        conv_p.append(cp[:, SUBLANES - (CONV_W - 1):])
        conv_s.append(cs)
    return (xp.reshape(batch, seq, D_MODEL), xs.reshape(dec_batch, dec_seq, D_MODEL),
            jnp.stack(swa_kp), jnp.stack(swa_vp), jnp.stack(sb_kp), jnp.stack(sb_vp), jnp.stack(conv_p),
            jnp.stack(swa_ks), jnp.stack(swa_vs), jnp.stack(sb_ks), jnp.stack(sb_vs), jnp.stack(conv_s))
```

```python
import functools

import jax
import jax.numpy as jnp
from jax import lax
from jax.experimental import pallas as pl
from jax.experimental.pallas import tpu as pltpu

F32 = jnp.float32
BF16 = jnp.bfloat16

D_MODEL = 1024
HEAD_DIM = 64
N_HEADS = 16
N_KV_HEADS = 4
GROUP = N_HEADS // N_KV_HEADS
D_KV = N_KV_HEADS * HEAD_DIM
DEC_SEQ = 8
WINDOW = 128
PAGE_SIZE = 128
ROPE_THETA = 10000.0
D_FF = 2816
CONV_W = 3
RMS_EPS = 1e-6
ATTN_SCALE = HEAD_DIM ** -0.5

LANES = 128
SUBLANES = 8
D_QX = N_HEADS * LANES
FF_CHUNK = 256
N_FF_CHUNKS = D_FF // FF_CHUNK
TOKEN_TILE = 512
SB_BLOCK = 2 * LANES
PAGES_PER_STEP = 16
PAGE_BUFFERS = 3
VMEM_LIMIT = 56 * 1024 * 1024

assert D_FF % FF_CHUNK == 0 and TOKEN_TILE % SB_BLOCK == 0
assert DEC_SEQ == SUBLANES and WINDOW == LANES and PAGE_SIZE == LANES and 2 * HEAD_DIM == LANES


def _params(n_axes):
    return pltpu.CompilerParams(dimension_semantics=("arbitrary",) * n_axes,
                                vmem_limit_bytes=VMEM_LIMIT)


def _resident(shape, index_map):
    return pl.BlockSpec(shape, index_map, pipeline_mode=pl.Buffered(1))


def _rmsnorm(x, g):
    r = lax.rsqrt(jnp.mean(x * x, axis=-1, keepdims=True) + RMS_EPS)
    return x * r * g


def _dot(a, b):
    return jnp.dot(a, b, preferred_element_type=F32)


def _dot_nt(a, b):
    return lax.dot_general(a, b, (((1,), (1,)), ((), ())), preferred_element_type=F32)


def _kv_pair(kv):
    return kv // 2, kv % 2


def _rope_tile(x, cosf, sinf):
    lane = lax.broadcasted_iota(jnp.int32, x.shape, 1)
    first = (lane % HEAD_DIM) < (HEAD_DIM // 2)
    other = jnp.where(first, pltpu.roll(x, LANES - HEAD_DIM // 2, axis=1),
                      pltpu.roll(x, HEAD_DIM // 2, axis=1))
    return x * cosf + other * sinf


def _rope_rows(x, cos_t, sin_t):
    half = HEAD_DIM // 2
    out = []
    for kv in range(N_KV_HEADS):
        x1 = x[kv * HEAD_DIM:kv * HEAD_DIM + half]
        x2 = x[kv * HEAD_DIM + half:(kv + 1) * HEAD_DIM]
        out += [x1 * cos_t - x2 * sin_t, x2 * cos_t + x1 * sin_t]
    return jnp.concatenate(out, axis=0)


def _qkv_prompt_body(*refs, rope, blocked):
    refs = list(refs)
    x_ref, g_ref, wq_ref, wk_ref, wv_ref = refs[:5]
    refs = refs[5:]
    if rope:
        cos_ref, sin_ref, cos_t_ref, sin_t_ref = refs[:4]
        refs = refs[4:]
    q_ref, k_ref, v_ref = refs[:3]
    h = _rmsnorm(x_ref[...], g_ref[...]).astype(BF16)
    tm = h.shape[0]
    q = _dot(h, wq_ref[...])
    k_t = _dot_nt(wk_ref[...], h)
    v_t = _dot_nt(wv_ref[...], h)
    if rope:
        cosf, sinf = cos_ref[...], sin_ref[...]
        for j in range(N_HEADS):
            sl = slice(j * LANES, (j + 1) * LANES)
            q_ref[:, sl] = _rope_tile(q[:, sl], cosf, sinf).astype(q_ref.dtype)
        k_t = _rope_rows(k_t, cos_t_ref[...], sin_t_ref[...])
    else:
        q_ref[...] = q.astype(q_ref.dtype)
    k_ref[...] = k_t.reshape(N_KV_HEADS, HEAD_DIM, tm)
    v_ref[...] = v_t.reshape(N_KV_HEADS, HEAD_DIM, tm)
    if blocked:
        kb_ref, vb_ref = refs[3:5]
        for c in range(tm // SB_BLOCK):
            sl = slice(c * SB_BLOCK, (c + 1) * SB_BLOCK)
            kb_ref[c] = k_t[:, sl].reshape(N_KV_HEADS, HEAD_DIM, SB_BLOCK)
            vb_ref[c] = v_t[:, sl].reshape(N_KV_HEADS, HEAD_DIM, SB_BLOCK)


def _qkv_prompt_call(x, g, wqx, wk_t, wv_t, tables, batch, seq, blocked, name):
    tm = TOKEN_TILE
    tps = seq // tm
    rope = tables is not None
    row = lambda i: (i, 0)
    const = lambda i: (0, 0)
    in_specs = [pl.BlockSpec((tm, D_MODEL), row),
                pl.BlockSpec((1, D_MODEL), const),
                _resident((D_MODEL, D_QX), const),
                _resident((D_KV, D_MODEL), const),
                _resident((D_KV, D_MODEL), const)]
    args = [x, g.reshape(1, D_MODEL), wqx, wk_t, wv_t]
    if rope:
        cosf, sinf, cos_t, sin_t = tables
        in_specs += [pl.BlockSpec((tm, LANES), lambda i: (i % tps, 0))] * 2
        in_specs += [pl.BlockSpec((HEAD_DIM // 2, tm), lambda i: (0, i % tps))] * 2
        args += [cosf, sinf, cos_t, sin_t]
    kv_spec = pl.BlockSpec((None, N_KV_HEADS, HEAD_DIM, tm), lambda i: (i // tps, 0, 0, i % tps))
    kv_shape = jax.ShapeDtypeStruct((batch, N_KV_HEADS, HEAD_DIM, seq), F32)
    out_specs = [pl.BlockSpec((tm, D_QX), row), kv_spec, kv_spec]
    out_shape = [jax.ShapeDtypeStruct((batch * seq, D_QX), BF16), kv_shape, kv_shape]
    if blocked:
        nb = tm // SB_BLOCK
        b_spec = pl.BlockSpec((None, nb, N_KV_HEADS, HEAD_DIM, SB_BLOCK), lambda i: (i // tps, i % tps, 0, 0, 0))
        b_shape = jax.ShapeDtypeStruct((batch, seq // SB_BLOCK, N_KV_HEADS, HEAD_DIM, SB_BLOCK), F32)
        out_specs += [b_spec, b_spec]
        out_shape += [b_shape, b_shape]
    return pl.pallas_call(
        functools.partial(_qkv_prompt_body, rope=rope, blocked=blocked),
        grid=(batch * seq // tm,),
        in_specs=in_specs, out_specs=out_specs, out_shape=out_shape,
        compiler_params=_params(1), name=name,
    )(*args)


def _qkv_sample_body(*refs, rope):
    if rope:
        x_ref, g_ref, wq_ref, wkv_ref, cos_ref, sin_ref, q_ref, k_ref, v_ref = refs
    else:
        x_ref, g_ref, wq_ref, wkv_ref, q_ref, k_ref, v_ref = refs
    h = _rmsnorm(x_ref[...], g_ref[...]).astype(BF16)
    q = _dot(h, wq_ref[...])
    kv = _dot(h, wkv_ref[...])
    if rope:
        cosf, sinf = cos_ref[...], sin_ref[...]
        for j in range(N_HEADS):
            sl = slice(j * LANES, (j + 1) * LANES)
            q_ref[:, sl] = _rope_tile(q[:, sl], cosf, sinf)
        for j in range(D_KV // LANES):
            sl = slice(j * LANES, (j + 1) * LANES)
            k_ref[:, sl] = _rope_tile(kv[:, sl], cosf, sinf)
    else:
        q_ref[...] = q
        k_ref[...] = kv[:, :D_KV]
    v_ref[...] = kv[:, D_KV:]


def _qkv_sample_call(x, g, wqx, wkv, tables, name):
    t = x.shape[0]
    tm = min(TOKEN_TILE, t)
    rope = tables is not None
    row = lambda i: (i, 0)
    const = lambda i: (0, 0)
    in_specs = [pl.BlockSpec((tm, D_MODEL), row),
                pl.BlockSpec((1, D_MODEL), const),
                _resident((D_MODEL, D_QX), const),
                _resident((D_MODEL, 2 * D_KV), const)]
    args = [x, g.reshape(1, D_MODEL), wqx, wkv]
    if rope:
        in_specs += [pl.BlockSpec((tm, LANES), const)] * 2
        args += list(tables)
    return pl.pallas_call(
        functools.partial(_qkv_sample_body, rope=rope),
        grid=(t // tm,),
        in_specs=in_specs,
        out_specs=[pl.BlockSpec((tm, D_QX), row), pl.BlockSpec((tm, D_KV), row), pl.BlockSpec((tm, D_KV), row)],
        out_shape=[jax.ShapeDtypeStruct((t, D_QX), F32),
                   jax.ShapeDtypeStruct((t, D_KV), F32),
                   jax.ShapeDtypeStruct((t, D_KV), F32)],
        compiler_params=_params(1), name=name,
    )(*args)


def _rope_tables(pos):
    inv_freq = ROPE_THETA ** (-jnp.arange(0, HEAD_DIM, 2, dtype=F32) / HEAD_DIM)
    ang = pos.astype(F32)[:, None] * inv_freq[None, :]
    cos, sin = jnp.cos(ang), jnp.sin(ang)
    cosf = jnp.concatenate([cos, cos, cos, cos], axis=1)
    sinf = jnp.concatenate([-sin, sin, -sin, sin], axis=1)
    return cosf, sinf, cos.T, sin.T


def _expand_q_weight(w_q, scale):
    w = (w_q * scale).reshape(D_MODEL, N_HEADS, 1, HEAD_DIM)
    half = ((jnp.arange(N_HEADS) // GROUP) % 2)[None, :, None, None]
    w = jnp.where(half == jnp.arange(2)[None, None, :, None], w, 0.0)
    return w.reshape(D_MODEL, D_QX).astype(BF16)


def _half_mask(shape, half):
    lane = lax.broadcasted_iota(jnp.int32, shape, 1)
    return (lane >= HEAD_DIM) if half else (lane < HEAD_DIM)


def _merge_heads(even, odd, even_half, odd_half):
    if even_half != 0:
        even = pltpu.roll(even, HEAD_DIM, axis=1)
    if odd_half != 1:
        odd = pltpu.roll(odd, HEAD_DIM, axis=1)
    return jnp.where(_half_mask(even.shape, 0), even, odd)


def _merge_all_heads(parts):
    tiles = [_merge_heads(parts[2 * p][0], parts[2 * p + 1][0], parts[2 * p][1], parts[2 * p + 1][1])
             for p in range(N_HEADS // 2)]
    return jnp.concatenate(tiles, axis=1)


def _block_diag_q(qx):
    zeros = jnp.zeros((SUBLANES, LANES), qx.dtype)
    rows = []
    for h in range(N_HEADS):
        pair, _ = _kv_pair(h // GROUP)
        tile = qx[:, h * LANES:(h + 1) * LANES]
        rows.append(jnp.concatenate([tile, zeros] if pair == 0 else [zeros, tile], axis=1))
    return jnp.concatenate(rows, axis=0)


def _unpack_block_diag(acc):
    parts = []
    for h in range(N_HEADS):
        pair, half = _kv_pair(h // GROUP)
        parts.append((acc[h * SUBLANES:(h + 1) * SUBLANES, pair * LANES:(pair + 1) * LANES], half))
    return _merge_all_heads(parts)


def _per_row_heads(scalar_ref, rows_per_head, width):
    return jnp.concatenate([jnp.full((rows_per_head, width), scalar_ref[h], F32) for h in range(N_HEADS)], axis=0)


def _pad_rows(pad_scr, x):
    pad_scr[...] = jnp.zeros(pad_scr.shape, F32)
    pad_scr[0:DEC_SEQ, :] = x
    return pad_scr[...]


def _swa_prompt_body(sink_ref, q_ref, kp_ref, kc_ref, vp_ref, vc_ref, o_ref):
    j = pl.program_id(1)
    rows = GROUP * WINDOW
    a = lax.broadcasted_iota(jnp.int32, (rows, 2 * WINDOW), 0) % WINDOW
    c = lax.broadcasted_iota(jnp.int32, (rows, 2 * WINDOW), 1)
    mask = (c >= a) & (c <= a + WINDOW) & ((j > 0) | (c >= WINDOW))
    parts = []
    for kv in range(N_KV_HEADS):
        pair, half = _kv_pair(kv)
        band = lambda prev, cur: jnp.concatenate(
            [prev[2 * pair:2 * pair + 2].reshape(LANES, WINDOW), cur[2 * pair:2 * pair + 2].reshape(LANES, WINDOW)],
            axis=1).astype(BF16)
        k_t = band(kp_ref, kc_ref)
        v_t = band(vp_ref, vc_ref)
        heads = range(kv * GROUP, (kv + 1) * GROUP)
        q4 = jnp.concatenate([q_ref[:, h * LANES:(h + 1) * LANES] for h in heads], axis=0)
        sink = jnp.concatenate([jnp.full((WINDOW, 1), sink_ref[h], F32) for h in heads], axis=0)
        z = jnp.where(mask, _dot(q4, k_t), -jnp.inf)
        m = jnp.maximum(jnp.max(z, axis=-1, keepdims=True), sink)
        p = jnp.exp(z - m)
        denom = jnp.sum(p, axis=-1, keepdims=True) + jnp.exp(sink - m)
        o4 = _dot_nt(p.astype(BF16), v_t) / denom
        parts += [(o4[g * WINDOW:(g + 1) * WINDOW], half) for g in range(GROUP)]
    o_ref[...] = _merge_all_heads(parts).astype(o_ref.dtype)


def _swa_prompt_call(sink, qx, k_t, v_t):
    batch, _, _, seq = k_t.shape
    nb = seq // WINDOW
    kv_spec = lambda f: pl.BlockSpec((None, N_KV_HEADS, HEAD_DIM, WINDOW), f)
    cur = lambda b, j: (b, 0, 0, j)
    prev = lambda b, j: (b, 0, 0, jnp.maximum(j - 1, 0))
    return pl.pallas_call(
        _swa_prompt_body,
        grid=(batch, nb),
        in_specs=[pl.BlockSpec(memory_space=pltpu.SMEM),
                  pl.BlockSpec((WINDOW, D_QX), lambda b, j: (b * nb + j, 0)),
                  kv_spec(prev), kv_spec(cur), kv_spec(prev), kv_spec(cur)],
        out_specs=pl.BlockSpec((WINDOW, D_MODEL), lambda b, j: (b * nb + j, 0)),
        out_shape=jax.ShapeDtypeStruct((batch * seq, D_MODEL), BF16),
        compiler_params=_params(2), name="swa_prompt",
    )(sink, qx, k_t, k_t, v_t, v_t)


def _swa_sample_body(sink_ref, q_ref, kn_ref, vn_ref, kc_ref, vc_ref, o_ref, kw_ref, vw_ref, pad_scr):
    rows = N_HEADS * DEC_SEQ
    t = lax.broadcasted_iota(jnp.int32, (rows, WINDOW), 0) % DEC_SEQ
    c = lax.broadcasted_iota(jnp.int32, (rows, WINDOW), 1)
    kc_t = kc_ref[...].reshape(D_KV, WINDOW)
    vc_t = vc_ref[...].reshape(D_KV, WINDOW)
    kn = _pad_rows(pad_scr, kn_ref[...])
    vn = _pad_rows(pad_scr, vn_ref[...])

    lane = lax.broadcasted_iota(jnp.int32, (D_KV, WINDOW), 1)
    keep = WINDOW - DEC_SEQ
    shift = lambda x: pltpu.roll(x, keep, axis=1)
    kw_ref[...] = jnp.where(lane < keep, shift(kc_t), shift(kn.T)).reshape(N_KV_HEADS, HEAD_DIM, WINDOW)
    vw_ref[...] = jnp.where(lane < keep, shift(vc_t), shift(vn.T)).reshape(N_KV_HEADS, HEAD_DIM, WINDOW)

    qbd = _block_diag_q(q_ref[...]).astype(BF16)
    sink = _per_row_heads(sink_ref, DEC_SEQ, 1)
    zc = jnp.where(c >= t, _dot(qbd, kc_t.astype(BF16)), -jnp.inf)
    zn = jnp.where(c <= t, _dot_nt(qbd, kn.astype(BF16)), -jnp.inf)
    m = jnp.maximum(jnp.maximum(jnp.max(zc, axis=-1, keepdims=True), jnp.max(zn, axis=-1, keepdims=True)), sink)
    pc = jnp.exp(zc - m)
    pn = jnp.exp(zn - m)
    denom = jnp.sum(pc, axis=-1, keepdims=True) + jnp.sum(pn, axis=-1, keepdims=True) + jnp.exp(sink - m)
    acc = (_dot_nt(pc.astype(BF16), vc_t.astype(BF16)) + _dot(pn.astype(BF16), vn.astype(BF16))) / denom
    o_ref[...] = _unpack_block_diag(acc)


def _swa_sample_call(sink, qx, k_new, v_new, k_cache_t, v_cache_t, layer):
    n = k_cache_t.shape[1]
    seq_rows = lambda s: (s, 0)
    cache_in = pl.BlockSpec((None, None, N_KV_HEADS, HEAD_DIM, WINDOW), lambda s: (layer, s, 0, 0, 0))
    cache_out = pl.BlockSpec((None, N_KV_HEADS, HEAD_DIM, WINDOW), lambda s: (s, 0, 0, 0))
    win_shape = jax.ShapeDtypeStruct((n, N_KV_HEADS, HEAD_DIM, WINDOW), F32)
    return pl.pallas_call(
        _swa_sample_body,
        grid=(n,),
        in_specs=[pl.BlockSpec(memory_space=pltpu.SMEM),
                  pl.BlockSpec((DEC_SEQ, D_QX), seq_rows),
                  pl.BlockSpec((DEC_SEQ, D_KV), seq_rows),
                  pl.BlockSpec((DEC_SEQ, D_KV), seq_rows),
                  cache_in, cache_in],
        out_specs=[pl.BlockSpec((DEC_SEQ, D_MODEL), seq_rows), cache_out, cache_out],
        out_shape=[jax.ShapeDtypeStruct((n * DEC_SEQ, D_MODEL), F32), win_shape, win_shape],
        scratch_shapes=[pltpu.VMEM((WINDOW, D_KV), F32)],
        compiler_params=_params(1), name="swa_sample",
    )(sink, qx, k_new, v_new, k_cache_t, v_cache_t)


def _suffix_matrix():
    j = jnp.arange(2 * LANES)[:, None] % LANES
    s = jnp.arange(2 * LANES)[None, :]
    return ((s >= LANES) | (j >= s)).astype(BF16)


def _hi_lo(x):
    hi = x.astype(BF16)
    lo = (x - hi.astype(F32)).astype(BF16)
    return jnp.concatenate([hi, lo], axis=1)


def _sb_blocks(neg_z_blocks, masks, u, carry):
    m = neg_z_blocks[0].shape[0]
    log_rest = []
    for zn, mk in zip(neg_z_blocks, masks):
        lr = jnp.minimum(zn, 0.0) - jnp.log(1.0 + jnp.exp(-jnp.abs(zn)))
        log_rest.append(lr if mk is None else jnp.where(mk, lr, 0.0))
    sums = _dot(jnp.concatenate([_hi_lo(x) for x in log_rest], axis=0), u)
    weights = []
    for n, (zn, mk) in enumerate(zip(neg_z_blocks, masks)):
        s = sums[n * m:(n + 1) * m]
        a = jnp.exp(s[:, :LANES] + carry - zn)
        weights.append(a if mk is None else jnp.where(mk, a, 0.0))
        carry = carry + s[:, LANES:]
    return weights, carry


def _sb_prompt_body(nbias_ref, q_ref, k_ref, v_ref, u_ref, o_ref, carry_scr, acc_scr):
    i = pl.program_id(1)
    rows = GROUP * SB_BLOCK
    sub = SB_BLOCK // LANES
    u = u_ref[...]
    t = lax.broadcasted_iota(jnp.int32, (rows, LANES), 0) % SB_BLOCK
    s = lax.broadcasted_iota(jnp.int32, (rows, LANES), 1)
    diag_masks = [(s + n * LANES) < t for n in reversed(range(sub))]

    def block(c, masks, first):
        for kv in range(N_KV_HEADS):
            pair, _ = _kv_pair(kv)
            heads = range(kv * GROUP, (kv + 1) * GROUP)
            q4 = jnp.concatenate([q_ref[:, h * LANES:(h + 1) * LANES] for h in heads], axis=0)
            nbias = jnp.concatenate([jnp.full((SB_BLOCK, LANES), nbias_ref[h], F32) for h in heads], axis=0)
            k_t = k_ref[c, 2 * pair:2 * pair + 2].reshape(LANES, SB_BLOCK).astype(BF16)
            v_t = v_ref[c, 2 * pair:2 * pair + 2].reshape(LANES, SB_BLOCK).astype(BF16)
            zn = _dot(q4, k_t)
            zs = [zn[:, n * LANES:(n + 1) * LANES] + nbias for n in reversed(range(sub))]
            carry = jnp.zeros((rows, LANES), F32) if first else carry_scr[kv]
            weights, carry = _sb_blocks(zs, masks, u, carry)
            a = jnp.concatenate(list(reversed(weights)), axis=1).astype(BF16)
            pv = _dot_nt(a, v_t)
            acc_scr[kv] = pv if first else acc_scr[kv] + pv
            carry_scr[kv] = carry

    block(i, diag_masks, True)
    lax.fori_loop(0, i, lambda n, _: (block(i - 1 - n, [None] * sub, False), 0)[1], 0)
    parts = []
    for kv in range(N_KV_HEADS):
        acc = acc_scr[kv]
        parts += [(acc[g * SB_BLOCK:(g + 1) * SB_BLOCK], _kv_pair(kv)[1]) for g in range(GROUP)]
    o_ref[...] = _merge_all_heads(parts).astype(o_ref.dtype)


def _sb_prompt_call(neg_bias, neg_qx, k_blocks, v_blocks, u):
    batch, nb = k_blocks.shape[:2]
    rows = GROUP * SB_BLOCK
    kv_spec = pl.BlockSpec((None, nb, N_KV_HEADS, HEAD_DIM, SB_BLOCK), lambda b, i: (b, 0, 0, 0, 0))
    return pl.pallas_call(
        _sb_prompt_body,
        grid=(batch, nb),
        in_specs=[pl.BlockSpec(memory_space=pltpu.SMEM),
                  pl.BlockSpec((SB_BLOCK, D_QX), lambda b, i: (b * nb + i, 0)),
                  kv_spec, kv_spec,
                  pl.BlockSpec((2 * LANES, 2 * LANES), lambda b, i: (0, 0))],
        out_specs=pl.BlockSpec((SB_BLOCK, D_MODEL), lambda b, i: (b * nb + i, 0)),
        out_shape=jax.ShapeDtypeStruct((batch * nb * SB_BLOCK, D_MODEL), BF16),
        scratch_shapes=[pltpu.VMEM((N_KV_HEADS, rows, LANES), F32),
                        pltpu.VMEM((N_KV_HEADS, rows, LANES), F32)],
        compiler_params=_params(2), name="sb_prompt",
    )(neg_bias, neg_qx, k_blocks, v_blocks, u)


def _page_copies(pt_ref, pool_k, pool_v, kbuf, vbuf, sems, layer, n_chunks, step, slot):
    n_pages = pt_ref.shape[1]
    seq, chunk = step // n_chunks, step % n_chunks
    copies = []
    for j in range(PAGES_PER_STEP):
        page = pt_ref[seq, n_pages - 1 - (chunk * PAGES_PER_STEP + j)]
        copies.append(pltpu.make_async_copy(pool_k.at[layer, page], kbuf.at[slot, j], sems.at[slot, 0]))
        copies.append(pltpu.make_async_copy(pool_v.at[layer, page], vbuf.at[slot, j], sems.at[slot, 1]))
    return copies


def _sb_sample_body(pt_ref, nbias_ref, q_ref, kn_ref, vn_ref, u_ref, pool_k, pool_v, o_ref,
                    qbd_scr, bias_scr, carry_scr, acc_scr, pad_scr, kbuf, vbuf, sems, *, layer, n_steps, n_chunks):
    c = pl.program_id(1)
    step = pl.program_id(0) * n_chunks + c
    slot = step % PAGE_BUFFERS
    rows = N_HEADS * DEC_SEQ
    u = u_ref[...]
    copies = functools.partial(_page_copies, pt_ref, pool_k, pool_v, kbuf, vbuf, sems, layer, n_chunks)

    @pl.when(step == 0)
    def _first_fetches():
        for d in range(PAGE_BUFFERS - 1):
            for cp in copies(d, d):
                cp.start()

    @pl.when(c == 0)
    def _new_tokens():
        qbd = _block_diag_q(q_ref[...]).astype(BF16)
        nbias = _per_row_heads(nbias_ref, DEC_SEQ, LANES)
        qbd_scr[...] = qbd
        bias_scr[...] = nbias
        kn = _pad_rows(pad_scr, kn_ref[...]).astype(BF16)
        vn = _pad_rows(pad_scr, vn_ref[...]).astype(BF16)
        t = lax.broadcasted_iota(jnp.int32, (rows, LANES), 0) % DEC_SEQ
        s = lax.broadcasted_iota(jnp.int32, (rows, LANES), 1)
        (a,), carry = _sb_blocks([_dot_nt(qbd, kn) + nbias], [s < t], u, jnp.zeros((rows, LANES), F32))
        carry_scr[...] = carry
        acc_scr[...] = _dot(a.astype(BF16), vn)

    ahead = step + (PAGE_BUFFERS - 1)
    for cp in copies(step, slot):
        cp.wait()
    for cp in copies(jnp.minimum(ahead, n_steps - 1), ahead % PAGE_BUFFERS):
        cp.start()

    qbd, nbias = qbd_scr[...], bias_scr[...]
    page = lambda buf, j: buf[slot, j].reshape(D_KV, PAGE_SIZE).astype(BF16)
    k_t = jnp.concatenate([page(kbuf, j) for j in range(PAGES_PER_STEP)], axis=1)
    v_t = jnp.concatenate([page(vbuf, j) for j in range(PAGES_PER_STEP)], axis=1)
    zn = _dot(qbd, k_t)
    zs = [zn[:, n * LANES:(n + 1) * LANES] + nbias for n in range(PAGES_PER_STEP)]
    weights, carry = _sb_blocks(zs, [None] * PAGES_PER_STEP, u, carry_scr[...])
    acc = acc_scr[...] + _dot_nt(jnp.concatenate(weights, axis=1).astype(BF16), v_t)
    carry_scr[...] = carry
    acc_scr[...] = acc

    @pl.when(c == n_chunks - 1)
    def _sequence_done():
        o_ref[...] = _unpack_block_diag(acc)

    @pl.when(step == n_steps - 1)
    def _drain():
        for d in range(1, PAGE_BUFFERS):
            for cp in copies(step, (step + d) % PAGE_BUFFERS):
                cp.wait()


def _sb_sample_call(page_table, neg_bias, neg_qx, k_new, v_new, u, pool_k_t, pool_v_t, layer):
    n, n_pages = page_table.shape
    assert n_pages % PAGES_PER_STEP == 0
    seq_rows = lambda s, c, pt: (s, 0)
    n_chunks = n_pages // PAGES_PER_STEP
    assert n * n_chunks >= PAGE_BUFFERS
    page_buf = pltpu.VMEM((PAGE_BUFFERS, PAGES_PER_STEP, N_KV_HEADS, HEAD_DIM, PAGE_SIZE), F32)
    rows = N_HEADS * DEC_SEQ
    grid_spec = pltpu.PrefetchScalarGridSpec(
        num_scalar_prefetch=1,
        grid=(n, n_chunks),
        in_specs=[pl.BlockSpec(memory_space=pltpu.SMEM),
                  pl.BlockSpec((DEC_SEQ, D_QX), seq_rows),
                  pl.BlockSpec((DEC_SEQ, D_KV), seq_rows),
                  pl.BlockSpec((DEC_SEQ, D_KV), seq_rows),
                  pl.BlockSpec((2 * LANES, 2 * LANES), lambda s, c, pt: (0, 0)),
                  pl.BlockSpec(memory_space=pl.ANY),
                  pl.BlockSpec(memory_space=pl.ANY)],
        out_specs=pl.BlockSpec((DEC_SEQ, D_MODEL), seq_rows),
        scratch_shapes=[pltpu.VMEM((rows, D_KV), BF16),
                        pltpu.VMEM((rows, LANES), F32),
                        pltpu.VMEM((rows, LANES), F32),
                        pltpu.VMEM((rows, D_KV), F32),
                        pltpu.VMEM((PAGE_SIZE, D_KV), F32),
                        page_buf, page_buf,
                        pltpu.SemaphoreType.DMA((PAGE_BUFFERS, 2))],
    )
    return pl.pallas_call(
        functools.partial(_sb_sample_body, layer=layer, n_steps=n * n_chunks, n_chunks=n_chunks),
        grid_spec=grid_spec,
        out_shape=jax.ShapeDtypeStruct((n * DEC_SEQ, D_MODEL), F32),
        compiler_params=_params(2), name="sb_sample",
    )(page_table, neg_bias, neg_qx, k_new, v_new, u, pool_k_t, pool_v_t)


def _ffn_body(*refs, sample, tiles_per_seq, final):
    refs = list(refs)
    x_ref, o_ref, wo_ref, g_ref, win_ref, cw_ref, cb_ref, wd_ref = refs[:8]
    refs = refs[8:]
    prev_ref = refs.pop(0) if sample else None
    gf_ref = refs.pop(0) if final else None
    xo_ref, conv_ref, h_scr, acc_scr = refs[:4]
    carry_scr = None if sample else refs[4]
    i = pl.program_id(0)
    tm = x_ref.shape[0]
    x1 = x_ref[...] + _dot(o_ref[...].astype(BF16), wo_ref[...])
    h_scr[...] = _rmsnorm(x1, g_ref[...]).astype(BF16)
    acc_scr[...] = x1
    if not sample:
        @pl.when(i % tiles_per_seq == 0)
        def _sequence_start():
            carry_scr[...] = jnp.zeros(carry_scr.shape, F32)

    def activation(c):
        sl = slice(c * FF_CHUNK, (c + 1) * FF_CHUNK)
        h = h_scr[...]
        gate = _dot(h, win_ref[:, sl])
        up = _dot(h, win_ref[:, D_FF + c * FF_CHUNK:D_FF + (c + 1) * FF_CHUNK])
        cw = cw_ref[:, sl]
        if sample:
            nseq = tm // DEC_SEQ
            prev = prev_ref[:, :, sl]
            expand = lambda p: jnp.broadcast_to(p, (nseq, DEC_SEQ, FF_CHUNK)).reshape(tm, FF_CHUNK)
            p0, p1 = expand(prev[:, 0:1, :]), expand(prev[:, 1:2, :])
            t = lax.broadcasted_iota(jnp.int32, (tm, FF_CHUNK), 0) % DEC_SEQ
            g1 = jnp.where(t == 0, p1, pltpu.roll(gate, 1, axis=0))
            g2 = jnp.where(t == 0, p0, jnp.where(t == 1, p1, pltpu.roll(gate, 2, axis=0)))
            conv_ref[:, :, sl] = gate.reshape(nseq, DEC_SEQ, FF_CHUNK)[:, DEC_SEQ - (CONV_W - 1):, :]
        else:
            prev8 = carry_scr[:, sl]
            top = gate[0:SUBLANES]
            r8 = lax.broadcasted_iota(jnp.int32, (SUBLANES, FF_CHUNK), 0)
            top1 = jnp.where(r8 < 1, pltpu.roll(prev8, 1, axis=0), pltpu.roll(top, 1, axis=0))
            top2 = jnp.where(r8 < 2, pltpu.roll(prev8, 2, axis=0), pltpu.roll(top, 2, axis=0))
            g1 = jnp.concatenate([top1, pltpu.roll(gate, 1, axis=0)[SUBLANES:]], axis=0)
            g2 = jnp.concatenate([top2, pltpu.roll(gate, 2, axis=0)[SUBLANES:]], axis=0)
            last8 = gate[tm - SUBLANES:]
            carry_scr[:, sl] = last8
            conv_ref[:, sl] = last8
        gc = cb_ref[:, sl] + cw[0:1] * g2 + cw[1:2] * g1 + cw[2:3] * gate
        return (gc / (1.0 + jnp.exp(-gc)) * up).astype(BF16)

    act = activation(0)
    for c in range(1, N_FF_CHUNKS):
        nxt = activation(c)
        acc_scr[...] += _dot(act, wd_ref[(c - 1) * FF_CHUNK:c * FF_CHUNK, :])
        act = nxt
    x2 = acc_scr[...] + _dot(act, wd_ref[(N_FF_CHUNKS - 1) * FF_CHUNK:, :])
    xo_ref[...] = _rmsnorm(x2, gf_ref[...]) if final else x2


def _ffn_call(x, o, wo, g, w_in, cw, cb, wd, prev, g_final, seq, name):
    t = x.shape[0]
    sample = prev is not None
    final = g_final is not None
    tm = min(TOKEN_TILE, t)
    tiles_per_seq = None if sample else seq // tm
    row = lambda i: (i, 0)
    const = lambda i: (0, 0)
    in_specs = [pl.BlockSpec((tm, D_MODEL), row),
                pl.BlockSpec((tm, D_MODEL), row),
                _resident((D_MODEL, D_MODEL), const),
                pl.BlockSpec((1, D_MODEL), const),
                _resident((D_MODEL, 2 * D_FF), const),
                pl.BlockSpec((CONV_W, D_FF), const),
                pl.BlockSpec((1, D_FF), const),
                _resident((D_FF, D_MODEL), const)]
    args = [x, o, wo, g.reshape(1, D_MODEL), w_in, cw, cb.reshape(1, D_FF), wd]
    scratch = [pltpu.VMEM((tm, D_MODEL), BF16), pltpu.VMEM((tm, D_MODEL), F32)]
    if sample:
        nseq = tm // DEC_SEQ
        state_spec = pl.BlockSpec((nseq, CONV_W - 1, D_FF), lambda i: (i, 0, 0))
        in_specs.append(state_spec)
        args.append(prev)
        conv_spec = state_spec
        conv_shape = jax.ShapeDtypeStruct((t // DEC_SEQ, CONV_W - 1, D_FF), F32)
    else:
        conv_spec = pl.BlockSpec((None, SUBLANES, D_FF), lambda i: (i // tiles_per_seq, 0, 0))
        conv_shape = jax.ShapeDtypeStruct((t // seq, SUBLANES, D_FF), F32)
        scratch.append(pltpu.VMEM((SUBLANES, D_FF), F32))
    if final:
        in_specs.append(pl.BlockSpec((1, D_MODEL), const))
        args.append(g_final.reshape(1, D_MODEL))
    return pl.pallas_call(
        functools.partial(_ffn_body, sample=sample, tiles_per_seq=tiles_per_seq, final=final),
        grid=(t // tm,),
        in_specs=in_specs,
        out_specs=[pl.BlockSpec((tm, D_MODEL), row), conv_spec],
        out_shape=[jax.ShapeDtypeStruct((t, D_MODEL), F32), conv_shape],
        scratch_shapes=scratch,
        compiler_params=_params(1), name=name,
    )(*args)


def _positions_last(x):
    return jnp.moveaxis(x, -3, -1)


def _positions_first(x):
    return jnp.moveaxis(x, -1, -3)


def kernel(x_prompt, x_sample, cache_swa_k, cache_swa_v, cache_sb_k, cache_sb_v, state_conv, page_table,
           norm_mix, norm_ffn, norm_final, w_qkv_a, sinks_a, w_o_a, w_qkv_b, bias_b, w_o_b,
           w_in_ffn, conv_w, conv_b, w_down):
    batch, seq, _ = x_prompt.shape
    dec_batch, dec_seq, _ = x_sample.shape
    depth = norm_mix.shape[0]
    past_len = page_table.shape[1] * cache_sb_k.shape[2]
    assert dec_seq == DEC_SEQ and cache_swa_k.shape[2] == WINDOW and cache_sb_k.shape[2] == PAGE_SIZE
    xp = x_prompt.reshape(batch * seq, D_MODEL)
    xs = x_sample.reshape(dec_batch * dec_seq, D_MODEL)
    rope_p = _rope_tables(jnp.arange(seq))
    rope_s = _rope_tables(past_len + jnp.arange(min(TOKEN_TILE, dec_batch * dec_seq)) % dec_seq)[:2]
    u = _suffix_matrix()
    swa_kc, swa_vc = _positions_last(cache_swa_k), _positions_last(cache_swa_v)
    sb_kc, sb_vc = _positions_last(cache_sb_k), _positions_last(cache_sb_v)

    swa_kp, swa_vp, swa_ks, swa_vs = [], [], [], []
    sb_kp, sb_vp, sb_ks, sb_vs = [], [], [], []
    conv_p, conv_s = [], []
    for i in range(depth):
        j = i // 2
        swa = i % 2 == 0
        w_qkv = (w_qkv_a if swa else w_qkv_b)[j]
        w_o = (w_o_a if swa else w_o_b)[j].astype(BF16)
        wqx = _expand_q_weight(w_qkv[:, :D_MODEL], ATTN_SCALE if swa else -ATTN_SCALE)
        wkv = w_qkv[:, D_MODEL:].astype(BF16)
        wk_t, wv_t = wkv[:, :D_KV].T, wkv[:, D_KV:].T
        qp, kp, vp, *blocks = _qkv_prompt_call(xp, norm_mix[i], wqx, wk_t, wv_t, rope_p if swa else None,
                                               batch, seq, not swa, f"qkv_prompt_{i}")
        qs, ks, vs = _qkv_sample_call(xs, norm_mix[i], wqx, wkv, rope_s if swa else None, f"qkv_sample_{i}")
        if swa:
            op = _swa_prompt_call(sinks_a[j], qp, kp, vp)
            os_, kw, vw = _swa_sample_call(sinks_a[j], qs, ks, vs, swa_kc, swa_vc, j)
            swa_kp.append(_positions_first(kp[..., seq - WINDOW:]))
            swa_vp.append(_positions_first(vp[..., seq - WINDOW:]))
            swa_ks.append(_positions_first(kw))
            swa_vs.append(_positions_first(vw))
        else:
            op = _sb_prompt_call(-bias_b[j], qp, blocks[0], blocks[1], u)
            os_ = _sb_sample_call(page_table, -bias_b[j], qs, ks, vs, u, sb_kc, sb_vc, j)
            sb_kp.append(_positions_first(kp))
            sb_vp.append(_positions_first(vp))
            sb_ks.append(ks.reshape(dec_batch, dec_seq, N_KV_HEADS, HEAD_DIM))
            sb_vs.append(vs.reshape(dec_batch, dec_seq, N_KV_HEADS, HEAD_DIM))
        g_final = norm_final if i == depth - 1 else None
        ffn_w = (w_o, norm_ffn[i], w_in_ffn[i].astype(BF16), conv_w[i], conv_b[i], w_down[i].astype(BF16))
        xp, cp = _ffn_call(xp, op, *ffn_w, None, g_final, seq, f"ffn_prompt_{i}")
        xs, cs = _ffn_call(xs, os_, *ffn_w, state_conv[i], g_final, None, f"ffn_sample_{i}")
        conv_p.append(cp[:, SUBLANES - (CONV_W - 1):])
        conv_s.append(cs)
    return (xp.reshape(batch, seq, D_MODEL), xs.reshape(dec_batch, dec_seq, D_MODEL),
            jnp.stack(swa_kp), jnp.stack(swa_vp), jnp.stack(sb_kp), jnp.stack(sb_vp), jnp.stack(conv_p),
            jnp.stack(swa_ks), jnp.stack(swa_vs), jnp.stack(sb_ks), jnp.stack(sb_vs), jnp.stack(conv_s))
```

```python
import functools

import jax
import jax.numpy as jnp
from jax import lax
from jax.experimental import pallas as pl
from jax.experimental.pallas import tpu as pltpu

F32 = jnp.float32
BF16 = jnp.bfloat16

D_MODEL = 1024
HEAD_DIM = 64
N_HEADS = 16
N_KV_HEADS = 4
GROUP = N_HEADS // N_KV_HEADS
D_KV = N_KV_HEADS * HEAD_DIM
DEC_SEQ = 8
WINDOW = 128
PAGE_SIZE = 128
ROPE_THETA = 10000.0
D_FF = 2816
CONV_W = 3
RMS_EPS = 1e-6
ATTN_SCALE = HEAD_DIM ** -0.5

LANES = 128
SUBLANES = 8
D_QX = N_HEADS * LANES
FF_CHUNK = 256
N_FF_CHUNKS = D_FF // FF_CHUNK
TOKEN_TILE = 512
FFN_TILE = 512
SB_BLOCK = 2 * LANES
PAGES_PER_STEP = 16
PAGE_BUFFERS = 3
VMEM_LIMIT = 56 * 1024 * 1024

assert D_FF % FF_CHUNK == 0 and TOKEN_TILE % SB_BLOCK == 0
assert DEC_SEQ == SUBLANES and WINDOW == LANES and PAGE_SIZE == LANES and 2 * HEAD_DIM == LANES


def _params(n_axes):
    return pltpu.CompilerParams(dimension_semantics=("arbitrary",) * n_axes,
                                vmem_limit_bytes=VMEM_LIMIT)


def _resident(shape, index_map):
    return pl.BlockSpec(shape, index_map, pipeline_mode=pl.Buffered(1))


def _rmsnorm(x, g):
    r = lax.rsqrt(jnp.mean(x * x, axis=-1, keepdims=True) + RMS_EPS)
    return x * r * g


def _dot(a, b):
    return jnp.dot(a, b, preferred_element_type=F32)


def _dot_nt(a, b):
    return lax.dot_general(a, b, (((1,), (1,)), ((), ())), preferred_element_type=F32)


def _kv_pair(kv):
    return kv // 2, kv % 2


def _rope_tile(x, cosf, sinf):
    lane = lax.broadcasted_iota(jnp.int32, x.shape, 1)
    first = (lane % HEAD_DIM) < (HEAD_DIM // 2)
    other = jnp.where(first, pltpu.roll(x, LANES - HEAD_DIM // 2, axis=1),
                      pltpu.roll(x, HEAD_DIM // 2, axis=1))
    return x * cosf + other * sinf


def _rope_rows(x, cos_t, sin_t):
    half = HEAD_DIM // 2
    out = []
    for kv in range(N_KV_HEADS):
        x1 = x[kv * HEAD_DIM:kv * HEAD_DIM + half]
        x2 = x[kv * HEAD_DIM + half:(kv + 1) * HEAD_DIM]
        out += [x1 * cos_t - x2 * sin_t, x2 * cos_t + x1 * sin_t]
    return jnp.concatenate(out, axis=0)


def _qkv_prompt_body(*refs, rope, blocked):
    refs = list(refs)
    x_ref, g_ref, wq_ref, wk_ref, wv_ref = refs[:5]
    refs = refs[5:]
    if rope:
        cos_ref, sin_ref, cos_t_ref, sin_t_ref = refs[:4]
        refs = refs[4:]
    q_ref, k_ref, v_ref = refs[:3]
    h = _rmsnorm(x_ref[...], g_ref[...]).astype(BF16)
    tm = h.shape[0]
    q = _dot(h, wq_ref[...])
    k_t = _dot_nt(wk_ref[...], h)
    v_t = _dot_nt(wv_ref[...], h)
    if rope:
        cosf, sinf = cos_ref[...], sin_ref[...]
        for j in range(N_HEADS):
            sl = slice(j * LANES, (j + 1) * LANES)
            q_ref[:, sl] = _rope_tile(q[:, sl], cosf, sinf).astype(q_ref.dtype)
        k_t = _rope_rows(k_t, cos_t_ref[...], sin_t_ref[...])
    else:
        q_ref[...] = q.astype(q_ref.dtype)
    k_ref[...] = k_t.reshape(N_KV_HEADS, HEAD_DIM, tm)
    v_ref[...] = v_t.reshape(N_KV_HEADS, HEAD_DIM, tm)
    if blocked:
        kb_ref, vb_ref = refs[3:5]
        for c in range(tm // SB_BLOCK):
            sl = slice(c * SB_BLOCK, (c + 1) * SB_BLOCK)
            kb_ref[c] = k_t[:, sl].reshape(N_KV_HEADS, HEAD_DIM, SB_BLOCK)
            vb_ref[c] = v_t[:, sl].reshape(N_KV_HEADS, HEAD_DIM, SB_BLOCK)


def _qkv_prompt_call(x, g, wqx, wk_t, wv_t, tables, batch, seq, blocked, name):
    tm = TOKEN_TILE
    tps = seq // tm
    rope = tables is not None
    row = lambda i: (i, 0)
    const = lambda i: (0, 0)
    in_specs = [pl.BlockSpec((tm, D_MODEL), row),
                pl.BlockSpec((1, D_MODEL), const),
                _resident((D_MODEL, D_QX), const),
                _resident((D_KV, D_MODEL), const),
                _resident((D_KV, D_MODEL), const)]
    args = [x, g.reshape(1, D_MODEL), wqx, wk_t, wv_t]
    if rope:
        cosf, sinf, cos_t, sin_t = tables
        in_specs += [pl.BlockSpec((tm, LANES), lambda i: (i % tps, 0))] * 2
        in_specs += [pl.BlockSpec((HEAD_DIM // 2, tm), lambda i: (0, i % tps))] * 2
        args += [cosf, sinf, cos_t, sin_t]
    kv_spec = pl.BlockSpec((None, N_KV_HEADS, HEAD_DIM, tm), lambda i: (i // tps, 0, 0, i % tps))
    kv_shape = jax.ShapeDtypeStruct((batch, N_KV_HEADS, HEAD_DIM, seq), F32)
    out_specs = [pl.BlockSpec((tm, D_QX), row), kv_spec, kv_spec]
    out_shape = [jax.ShapeDtypeStruct((batch * seq, D_QX), BF16), kv_shape, kv_shape]
    if blocked:
        nb = tm // SB_BLOCK
        b_spec = pl.BlockSpec((None, nb, N_KV_HEADS, HEAD_DIM, SB_BLOCK), lambda i: (i // tps, i % tps, 0, 0, 0))
        b_shape = jax.ShapeDtypeStruct((batch, seq // SB_BLOCK, N_KV_HEADS, HEAD_DIM, SB_BLOCK), F32)
        out_specs += [b_spec, b_spec]
        out_shape += [b_shape, b_shape]
    return pl.pallas_call(
        functools.partial(_qkv_prompt_body, rope=rope, blocked=blocked),
        grid=(batch * seq // tm,),
        in_specs=in_specs, out_specs=out_specs, out_shape=out_shape,
        compiler_params=_params(1), name=name,
    )(*args)


def _qkv_sample_body(*refs, rope):
    if rope:
        x_ref, g_ref, wq_ref, wkv_ref, cos_ref, sin_ref, q_ref, k_ref, v_ref = refs
    else:
        x_ref, g_ref, wq_ref, wkv_ref, q_ref, k_ref, v_ref = refs
    h = _rmsnorm(x_ref[...], g_ref[...]).astype(BF16)
    q = _dot(h, wq_ref[...])
    kv = _dot(h, wkv_ref[...])
    if rope:
        cosf, sinf = cos_ref[...], sin_ref[...]
        for j in range(N_HEADS):
            sl = slice(j * LANES, (j + 1) * LANES)
            q_ref[:, sl] = _rope_tile(q[:, sl], cosf, sinf)
        for j in range(D_KV // LANES):
            sl = slice(j * LANES, (j + 1) * LANES)
            k_ref[:, sl] = _rope_tile(kv[:, sl], cosf, sinf)
    else:
        q_ref[...] = q
        k_ref[...] = kv[:, :D_KV]
    v_ref[...] = kv[:, D_KV:]


def _qkv_sample_call(x, g, wqx, wkv, tables, name):
    t = x.shape[0]
    tm = min(TOKEN_TILE, t)
    rope = tables is not None
    row = lambda i: (i, 0)
    const = lambda i: (0, 0)
    in_specs = [pl.BlockSpec((tm, D_MODEL), row),
                pl.BlockSpec((1, D_MODEL), const),
                _resident((D_MODEL, D_QX), const),
                _resident((D_MODEL, 2 * D_KV), const)]
    args = [x, g.reshape(1, D_MODEL), wqx, wkv]
    if rope:
        in_specs += [pl.BlockSpec((tm, LANES), const)] * 2
        args += list(tables)
    return pl.pallas_call(
        functools.partial(_qkv_sample_body, rope=rope),
        grid=(t // tm,),
        in_specs=in_specs,
        out_specs=[pl.BlockSpec((tm, D_QX), row), pl.BlockSpec((tm, D_KV), row), pl.BlockSpec((tm, D_KV), row)],
        out_shape=[jax.ShapeDtypeStruct((t, D_QX), F32),
                   jax.ShapeDtypeStruct((t, D_KV), F32),
                   jax.ShapeDtypeStruct((t, D_KV), F32)],
        compiler_params=_params(1), name=name,
    )(*args)


def _rope_tables(pos):
    inv_freq = ROPE_THETA ** (-jnp.arange(0, HEAD_DIM, 2, dtype=F32) / HEAD_DIM)
    ang = pos.astype(F32)[:, None] * inv_freq[None, :]
    cos, sin = jnp.cos(ang), jnp.sin(ang)
    cosf = jnp.concatenate([cos, cos, cos, cos], axis=1)
    sinf = jnp.concatenate([-sin, sin, -sin, sin], axis=1)
    return cosf, sinf, cos.T, sin.T


def _expand_q_weight(w_q, scale):
    w = (w_q * scale).reshape(D_MODEL, N_HEADS, 1, HEAD_DIM)
    half = ((jnp.arange(N_HEADS) // GROUP) % 2)[None, :, None, None]
    w = jnp.where(half == jnp.arange(2)[None, None, :, None], w, 0.0)
    return w.reshape(D_MODEL, D_QX).astype(BF16)


def _half_mask(shape, half):
    lane = lax.broadcasted_iota(jnp.int32, shape, 1)
    return (lane >= HEAD_DIM) if half else (lane < HEAD_DIM)


def _merge_heads(even, odd, even_half, odd_half):
    if even_half != 0:
        even = pltpu.roll(even, HEAD_DIM, axis=1)
    if odd_half != 1:
        odd = pltpu.roll(odd, HEAD_DIM, axis=1)
    return jnp.where(_half_mask(even.shape, 0), even, odd)


def _merge_all_heads(parts):
    tiles = [_merge_heads(parts[2 * p][0], parts[2 * p + 1][0], parts[2 * p][1], parts[2 * p + 1][1])
             for p in range(N_HEADS // 2)]
    return jnp.concatenate(tiles, axis=1)


def _block_diag_q(qx):
    zeros = jnp.zeros((SUBLANES, LANES), qx.dtype)
    rows = []
    for h in range(N_HEADS):
        pair, _ = _kv_pair(h // GROUP)
        tile = qx[:, h * LANES:(h + 1) * LANES]
        rows.append(jnp.concatenate([tile, zeros] if pair == 0 else [zeros, tile], axis=1))
    return jnp.concatenate(rows, axis=0)


def _unpack_block_diag(acc):
    parts = []
    for h in range(N_HEADS):
        pair, half = _kv_pair(h // GROUP)
        parts.append((acc[h * SUBLANES:(h + 1) * SUBLANES, pair * LANES:(pair + 1) * LANES], half))
    return _merge_all_heads(parts)


def _per_row_heads(scalar_ref, rows_per_head, width):
    return jnp.concatenate([jnp.full((rows_per_head, width), scalar_ref[h], F32) for h in range(N_HEADS)], axis=0)


def _pad_rows(pad_scr, x):
    pad_scr[...] = jnp.zeros(pad_scr.shape, F32)
    pad_scr[0:DEC_SEQ, :] = x
    return pad_scr[...]


def _swa_prompt_body(sink_ref, q_ref, kp_ref, kc_ref, vp_ref, vc_ref, o_ref):
    j = pl.program_id(1)
    rows = GROUP * WINDOW
    a = lax.broadcasted_iota(jnp.int32, (rows, 2 * WINDOW), 0) % WINDOW
    c = lax.broadcasted_iota(jnp.int32, (rows, 2 * WINDOW), 1)
    mask = (c >= a) & (c <= a + WINDOW) & ((j > 0) | (c >= WINDOW))
    parts = []
    for kv in range(N_KV_HEADS):
        pair, half = _kv_pair(kv)
        band = lambda prev, cur: jnp.concatenate(
            [prev[2 * pair:2 * pair + 2].reshape(LANES, WINDOW), cur[2 * pair:2 * pair + 2].reshape(LANES, WINDOW)],
            axis=1).astype(BF16)
        k_t = band(kp_ref, kc_ref)
        v_t = band(vp_ref, vc_ref)
        heads = range(kv * GROUP, (kv + 1) * GROUP)
        q4 = jnp.concatenate([q_ref[:, h * LANES:(h + 1) * LANES] for h in heads], axis=0)
        sink = jnp.concatenate([jnp.full((WINDOW, LANES), sink_ref[h], F32) for h in heads], axis=0)
        z = jnp.where(mask, _dot(q4, k_t), -jnp.inf)
        m = jnp.maximum(jnp.max(z, axis=-1, keepdims=True), sink)
        p = jnp.exp(z - jnp.concatenate([m, m], axis=1)).astype(BF16)
        denom = _dot(p, jnp.ones((2 * WINDOW, LANES), BF16)) + jnp.exp(sink - m)
        o4 = _dot_nt(p, v_t) / denom
        parts += [(o4[g * WINDOW:(g + 1) * WINDOW], half) for g in range(GROUP)]
    o_ref[...] = _merge_all_heads(parts).astype(o_ref.dtype)


def _swa_prompt_call(sink, qx, k_t, v_t):
    batch, _, _, seq = k_t.shape
    nb = seq // WINDOW
    kv_spec = lambda f: pl.BlockSpec((None, N_KV_HEADS, HEAD_DIM, WINDOW), f)
    cur = lambda b, j: (b, 0, 0, j)
    prev = lambda b, j: (b, 0, 0, jnp.maximum(j - 1, 0))
    return pl.pallas_call(
        _swa_prompt_body,
        grid=(batch, nb),
        in_specs=[pl.BlockSpec(memory_space=pltpu.SMEM),
                  pl.BlockSpec((WINDOW, D_QX), lambda b, j: (b * nb + j, 0)),
                  kv_spec(prev), kv_spec(cur), kv_spec(prev), kv_spec(cur)],
        out_specs=pl.BlockSpec((WINDOW, D_MODEL), lambda b, j: (b * nb + j, 0)),
        out_shape=jax.ShapeDtypeStruct((batch * seq, D_MODEL), BF16),
        compiler_params=_params(2), name="swa_prompt",
    )(sink, qx, k_t, k_t, v_t, v_t)


def _swa_sample_body(sink_ref, q_ref, kn_ref, vn_ref, kc_ref, vc_ref, o_ref, kw_ref, vw_ref, pad_scr):
    rows = N_HEADS * DEC_SEQ
    t = lax.broadcasted_iota(jnp.int32, (rows, WINDOW), 0) % DEC_SEQ
    c = lax.broadcasted_iota(jnp.int32, (rows, WINDOW), 1)
    kc_t = kc_ref[...].reshape(D_KV, WINDOW)
    vc_t = vc_ref[...].reshape(D_KV, WINDOW)
    kn = _pad_rows(pad_scr, kn_ref[...])
    vn = _pad_rows(pad_scr, vn_ref[...])

    lane = lax.broadcasted_iota(jnp.int32, (D_KV, WINDOW), 1)
    keep = WINDOW - DEC_SEQ
    shift = lambda x: pltpu.roll(x, keep, axis=1)
    kw_ref[...] = jnp.where(lane < keep, shift(kc_t), shift(kn.T)).reshape(N_KV_HEADS, HEAD_DIM, WINDOW)
    vw_ref[...] = jnp.where(lane < keep, shift(vc_t), shift(vn.T)).reshape(N_KV_HEADS, HEAD_DIM, WINDOW)

    qbd = _block_diag_q(q_ref[...]).astype(BF16)
    sink = _per_row_heads(sink_ref, DEC_SEQ, 1)
    zc = jnp.where(c >= t, _dot(qbd, kc_t.astype(BF16)), -jnp.inf)
    zn = jnp.where(c <= t, _dot_nt(qbd, kn.astype(BF16)), -jnp.inf)
    m = jnp.maximum(jnp.maximum(jnp.max(zc, axis=-1, keepdims=True), jnp.max(zn, axis=-1, keepdims=True)), sink)
    pc = jnp.exp(zc - m)
    pn = jnp.exp(zn - m)
    denom = jnp.sum(pc, axis=-1, keepdims=True) + jnp.sum(pn, axis=-1, keepdims=True) + jnp.exp(sink - m)
    acc = (_dot_nt(pc.astype(BF16), vc_t.astype(BF16)) + _dot(pn.astype(BF16), vn.astype(BF16))) / denom
    o_ref[...] = _unpack_block_diag(acc)


def _swa_sample_call(sink, qx, k_new, v_new, k_cache_t, v_cache_t, layer):
    n = k_cache_t.shape[1]
    seq_rows = lambda s: (s, 0)
    cache_in = pl.BlockSpec((None, None, N_KV_HEADS, HEAD_DIM, WINDOW), lambda s: (layer, s, 0, 0, 0))
    cache_out = pl.BlockSpec((None, N_KV_HEADS, HEAD_DIM, WINDOW), lambda s: (s, 0, 0, 0))
    win_shape = jax.ShapeDtypeStruct((n, N_KV_HEADS, HEAD_DIM, WINDOW), F32)
    return pl.pallas_call(
        _swa_sample_body,
        grid=(n,),
        in_specs=[pl.BlockSpec(memory_space=pltpu.SMEM),
                  pl.BlockSpec((DEC_SEQ, D_QX), seq_rows),
                  pl.BlockSpec((DEC_SEQ, D_KV), seq_rows),
                  pl.BlockSpec((DEC_SEQ, D_KV), seq_rows),
                  cache_in, cache_in],
        out_specs=[pl.BlockSpec((DEC_SEQ, D_MODEL), seq_rows), cache_out, cache_out],
        out_shape=[jax.ShapeDtypeStruct((n * DEC_SEQ, D_MODEL), F32), win_shape, win_shape],
        scratch_shapes=[pltpu.VMEM((WINDOW, D_KV), F32)],
        compiler_params=_params(1), name="swa_sample",
    )(sink, qx, k_new, v_new, k_cache_t, v_cache_t)


def _suffix_matrix():
    j = jnp.arange(2 * LANES)[:, None] % LANES
    s = jnp.arange(2 * LANES)[None, :]
    return ((s >= LANES) | (j >= s)).astype(BF16)


def _hi_lo(x):
    hi = x.astype(BF16)
    lo = (x - hi.astype(F32)).astype(BF16)
    return jnp.concatenate([hi, lo], axis=1)


def _sb_log_rest(neg_z_blocks, masks):
    rows = []
    for zn, mk in zip(neg_z_blocks, masks):
        lr = jnp.minimum(zn, 0.0) - jnp.log(1.0 + jnp.exp(-jnp.abs(zn)))
        rows.append(_hi_lo(lr if mk is None else jnp.where(mk, lr, 0.0)))
    return jnp.concatenate(rows, axis=0)


def _sb_weights(neg_z_blocks, masks, sums, carry):
    m = neg_z_blocks[0].shape[0]
    weights = []
    for n, (zn, mk) in enumerate(zip(neg_z_blocks, masks)):
        s = sums[n * m:(n + 1) * m]
        a = jnp.exp(s[:, :LANES] + carry - zn)
        weights.append(a if mk is None else jnp.where(mk, a, 0.0))
        carry = carry + s[:, LANES:]
    return weights, carry


def _sb_blocks(neg_z_blocks, masks, u, carry):
    sums = _dot(_sb_log_rest(neg_z_blocks, masks), u)
    return _sb_weights(neg_z_blocks, masks, sums, carry)


def _sb_prompt_body(nbias_ref, q_ref, k_ref, v_ref, u_ref, o_ref, carry_scr, acc_scr):
    i = pl.program_id(1)
    rows = GROUP * SB_BLOCK
    sub = SB_BLOCK // LANES
    u = u_ref[...]
    t = lax.broadcasted_iota(jnp.int32, (rows, LANES), 0) % SB_BLOCK
    s = lax.broadcasted_iota(jnp.int32, (rows, LANES), 1)
    diag_masks = [(s + n * LANES) < t for n in reversed(range(sub))]

    def block(c, masks, first):
        kvs = range(N_KV_HEADS)
        tile = lambda ref, kv: ref[c, 2 * (kv // 2):2 * (kv // 2) + 2].reshape(LANES, SB_BLOCK).astype(BF16)
        zs = []
        for kv in kvs:
            heads = range(kv * GROUP, (kv + 1) * GROUP)
            q4 = jnp.concatenate([q_ref[:, h * LANES:(h + 1) * LANES] for h in heads], axis=0)
            nbias = jnp.concatenate([jnp.full((SB_BLOCK, LANES), nbias_ref[h], F32) for h in heads], axis=0)
            zn = _dot(q4, tile(k_ref, kv))
            zs.append([zn[:, n * LANES:(n + 1) * LANES] + nbias for n in reversed(range(sub))])
        log_rest = [_sb_log_rest(zs[kv], masks) for kv in kvs]
        sums = [_dot(log_rest[kv], u) for kv in kvs]
        for kv in kvs:
            carry = jnp.zeros((rows, LANES), F32) if first else carry_scr[kv]
            weights, carry = _sb_weights(zs[kv], masks, sums[kv], carry)
            a = jnp.concatenate(list(reversed(weights)), axis=1).astype(BF16)
            pv = _dot_nt(a, tile(v_ref, kv))
            acc_scr[kv] = pv if first else acc_scr[kv] + pv
            carry_scr[kv] = carry

    block(i, diag_masks, True)
    lax.fori_loop(0, i, lambda n, _: (block(i - 1 - n, [None] * sub, False), 0)[1], 0)
    parts = []
    for kv in range(N_KV_HEADS):
        acc = acc_scr[kv]
        parts += [(acc[g * SB_BLOCK:(g + 1) * SB_BLOCK], _kv_pair(kv)[1]) for g in range(GROUP)]
    o_ref[...] = _merge_all_heads(parts).astype(o_ref.dtype)


def _sb_prompt_call(neg_bias, neg_qx, k_blocks, v_blocks, u):
    batch, nb = k_blocks.shape[:2]
    rows = GROUP * SB_BLOCK
    kv_spec = pl.BlockSpec((None, nb, N_KV_HEADS, HEAD_DIM, SB_BLOCK), lambda b, i: (b, 0, 0, 0, 0))
    return pl.pallas_call(
        _sb_prompt_body,
        grid=(batch, nb),
        in_specs=[pl.BlockSpec(memory_space=pltpu.SMEM),
                  pl.BlockSpec((SB_BLOCK, D_QX), lambda b, i: (b * nb + i, 0)),
                  kv_spec, kv_spec,
                  pl.BlockSpec((2 * LANES, 2 * LANES), lambda b, i: (0, 0))],
        out_specs=pl.BlockSpec((SB_BLOCK, D_MODEL), lambda b, i: (b * nb + i, 0)),
        out_shape=jax.ShapeDtypeStruct((batch * nb * SB_BLOCK, D_MODEL), BF16),
        scratch_shapes=[pltpu.VMEM((N_KV_HEADS, rows, LANES), F32),
                        pltpu.VMEM((N_KV_HEADS, rows, LANES), F32)],
        compiler_params=_params(2), name="sb_prompt",
    )(neg_bias, neg_qx, k_blocks, v_blocks, u)


def _page_copies(pt_ref, pool_k, pool_v, kbuf, vbuf, sems, layer, n_chunks, step, slot):
    n_pages = pt_ref.shape[1]
    seq, chunk = step // n_chunks, step % n_chunks
    copies = []
    for j in range(PAGES_PER_STEP):
        page = pt_ref[seq, n_pages - 1 - (chunk * PAGES_PER_STEP + j)]
        copies.append(pltpu.make_async_copy(pool_k.at[layer, page], kbuf.at[slot, j], sems.at[slot, 0]))
        copies.append(pltpu.make_async_copy(pool_v.at[layer, page], vbuf.at[slot, j], sems.at[slot, 1]))
    return copies


def _sb_sample_body(pt_ref, nbias_ref, q_ref, kn_ref, vn_ref, u_ref, pool_k, pool_v, o_ref,
                    qbd_scr, bias_scr, carry_scr, acc_scr, pad_scr, kbuf, vbuf, sems, *, layer, n_steps, n_chunks):
    c = pl.program_id(1)
    step = pl.program_id(0) * n_chunks + c
    slot = step % PAGE_BUFFERS
    rows = N_HEADS * DEC_SEQ
    u = u_ref[...]
    copies = functools.partial(_page_copies, pt_ref, pool_k, pool_v, kbuf, vbuf, sems, layer, n_chunks)

    @pl.when(step == 0)
    def _first_fetches():
        for d in range(PAGE_BUFFERS - 1):
            for cp in copies(d, d):
                cp.start()

    @pl.when(c == 0)
    def _new_tokens():
        qbd = _block_diag_q(q_ref[...]).astype(BF16)
        nbias = _per_row_heads(nbias_ref, DEC_SEQ, LANES)
        qbd_scr[...] = qbd
        bias_scr[...] = nbias
        kn = _pad_rows(pad_scr, kn_ref[...]).astype(BF16)
        vn = _pad_rows(pad_scr, vn_ref[...]).astype(BF16)
        t = lax.broadcasted_iota(jnp.int32, (rows, LANES), 0) % DEC_SEQ
        s = lax.broadcasted_iota(jnp.int32, (rows, LANES), 1)
        (a,), carry = _sb_blocks([_dot_nt(qbd, kn) + nbias], [s < t], u, jnp.zeros((rows, LANES), F32))
        carry_scr[...] = carry
        acc_scr[...] = _dot(a.astype(BF16), vn)

    ahead = step + (PAGE_BUFFERS - 1)
    for cp in copies(step, slot):
        cp.wait()
    for cp in copies(jnp.minimum(ahead, n_steps - 1), ahead % PAGE_BUFFERS):
        cp.start()

    qbd, nbias = qbd_scr[...], bias_scr[...]
    page = lambda buf, j: buf[slot, j].reshape(D_KV, PAGE_SIZE).astype(BF16)
    k_t = jnp.concatenate([page(kbuf, j) for j in range(PAGES_PER_STEP)], axis=1)
    v_t = jnp.concatenate([page(vbuf, j) for j in range(PAGES_PER_STEP)], axis=1)
    zn = _dot(qbd, k_t)
    zs = [zn[:, n * LANES:(n + 1) * LANES] + nbias for n in range(PAGES_PER_STEP)]
    weights, carry = _sb_blocks(zs, [None] * PAGES_PER_STEP, u, carry_scr[...])
    acc = acc_scr[...] + _dot_nt(jnp.concatenate(weights, axis=1).astype(BF16), v_t)
    carry_scr[...] = carry
    acc_scr[...] = acc

    @pl.when(c == n_chunks - 1)
    def _sequence_done():
        o_ref[...] = _unpack_block_diag(acc)

    @pl.when(step == n_steps - 1)
    def _drain():
        for d in range(1, PAGE_BUFFERS):
            for cp in copies(step, (step + d) % PAGE_BUFFERS):
                cp.wait()


def _sb_sample_call(page_table, neg_bias, neg_qx, k_new, v_new, u, pool_k_t, pool_v_t, layer):
    n, n_pages = page_table.shape
    assert n_pages % PAGES_PER_STEP == 0
    seq_rows = lambda s, c, pt: (s, 0)
    n_chunks = n_pages // PAGES_PER_STEP
    assert n * n_chunks >= PAGE_BUFFERS
    page_buf = pltpu.VMEM((PAGE_BUFFERS, PAGES_PER_STEP, N_KV_HEADS, HEAD_DIM, PAGE_SIZE), F32)
    rows = N_HEADS * DEC_SEQ
    grid_spec = pltpu.PrefetchScalarGridSpec(
        num_scalar_prefetch=1,
        grid=(n, n_chunks),
        in_specs=[pl.BlockSpec(memory_space=pltpu.SMEM),
                  pl.BlockSpec((DEC_SEQ, D_QX), seq_rows),
                  pl.BlockSpec((DEC_SEQ, D_KV), seq_rows),
                  pl.BlockSpec((DEC_SEQ, D_KV), seq_rows),
                  pl.BlockSpec((2 * LANES, 2 * LANES), lambda s, c, pt: (0, 0)),
                  pl.BlockSpec(memory_space=pl.ANY),
                  pl.BlockSpec(memory_space=pl.ANY)],
        out_specs=pl.BlockSpec((DEC_SEQ, D_MODEL), seq_rows),
        scratch_shapes=[pltpu.VMEM((rows, D_KV), BF16),
                        pltpu.VMEM((rows, LANES), F32),
                        pltpu.VMEM((rows, LANES), F32),
                        pltpu.VMEM((rows, D_KV), F32),
                        pltpu.VMEM((PAGE_SIZE, D_KV), F32),
                        page_buf, page_buf,
                        pltpu.SemaphoreType.DMA((PAGE_BUFFERS, 2))],
    )
    return pl.pallas_call(
        functools.partial(_sb_sample_body, layer=layer, n_steps=n * n_chunks, n_chunks=n_chunks),
        grid_spec=grid_spec,
        out_shape=jax.ShapeDtypeStruct((n * DEC_SEQ, D_MODEL), F32),
        compiler_params=_params(2), name="sb_sample",
    )(page_table, neg_bias, neg_qx, k_new, v_new, u, pool_k_t, pool_v_t)


def _ffn_body(*refs, sample, tiles_per_seq, final):
    refs = list(refs)
    x_ref, o_ref, wo_ref, g_ref, win_ref, cw_ref, cb_ref, wd_ref = refs[:8]
    refs = refs[8:]
    prev_ref = refs.pop(0) if sample else None
    gf_ref = refs.pop(0) if final else None
    xo_ref, conv_ref, h_scr, acc_scr = refs[:4]
    carry_scr = None if sample else refs[4]
    i = pl.program_id(0)
    tm = x_ref.shape[0]
    x1 = x_ref[...] + _dot(o_ref[...].astype(BF16), wo_ref[...])
    h_scr[...] = _rmsnorm(x1, g_ref[...]).astype(BF16)
    acc_scr[...] = x1
    if not sample:
        @pl.when(i % tiles_per_seq == 0)
        def _sequence_start():
            carry_scr[...] = jnp.zeros(carry_scr.shape, F32)

    def activation(c):
        sl = slice(c * FF_CHUNK, (c + 1) * FF_CHUNK)
        h = h_scr[...]
        gate = _dot(h, win_ref[:, sl])
        up = _dot(h, win_ref[:, D_FF + c * FF_CHUNK:D_FF + (c + 1) * FF_CHUNK])
        cw = cw_ref[:, sl]
        if sample:
            nseq = tm // DEC_SEQ
            prev = prev_ref[:, :, sl]
            expand = lambda p: jnp.broadcast_to(p, (nseq, DEC_SEQ, FF_CHUNK)).reshape(tm, FF_CHUNK)
            p0, p1 = expand(prev[:, 0:1, :]), expand(prev[:, 1:2, :])
            t = lax.broadcasted_iota(jnp.int32, (tm, FF_CHUNK), 0) % DEC_SEQ
            g1 = jnp.where(t == 0, p1, pltpu.roll(gate, 1, axis=0))
            g2 = jnp.where(t == 0, p0, jnp.where(t == 1, p1, pltpu.roll(gate, 2, axis=0)))
            conv_ref[:, :, sl] = gate.reshape(nseq, DEC_SEQ, FF_CHUNK)[:, DEC_SEQ - (CONV_W - 1):, :]
        else:
            prev8 = carry_scr[:, sl]
            top = gate[0:SUBLANES]
            r8 = lax.broadcasted_iota(jnp.int32, (SUBLANES, FF_CHUNK), 0)
            top1 = jnp.where(r8 < 1, pltpu.roll(prev8, 1, axis=0), pltpu.roll(top, 1, axis=0))
            top2 = jnp.where(r8 < 2, pltpu.roll(prev8, 2, axis=0), pltpu.roll(top, 2, axis=0))
            g1 = jnp.concatenate([top1, pltpu.roll(gate, 1, axis=0)[SUBLANES:]], axis=0)
            g2 = jnp.concatenate([top2, pltpu.roll(gate, 2, axis=0)[SUBLANES:]], axis=0)
            last8 = gate[tm - SUBLANES:]
            carry_scr[:, sl] = last8
            conv_ref[:, sl] = last8
        gc = cb_ref[:, sl] + cw[0:1] * g2 + cw[1:2] * g1 + cw[2:3] * gate
        return (gc / (1.0 + jnp.exp(-gc)) * up).astype(BF16)

    act = activation(0)
    for c in range(1, N_FF_CHUNKS):
        nxt = activation(c)
        acc_scr[...] += _dot(act, wd_ref[(c - 1) * FF_CHUNK:c * FF_CHUNK, :])
        act = nxt
    x2 = acc_scr[...] + _dot(act, wd_ref[(N_FF_CHUNKS - 1) * FF_CHUNK:, :])
    xo_ref[...] = _rmsnorm(x2, gf_ref[...]) if final else x2


def _ffn_call(x, o, wo, g, w_in, cw, cb, wd, prev, g_final, seq, name):
    t = x.shape[0]
    sample = prev is not None
    final = g_final is not None
    tm = min(FFN_TILE, t)
    tiles_per_seq = None if sample else seq // tm
    row = lambda i: (i, 0)
    const = lambda i: (0, 0)
    in_specs = [pl.BlockSpec((tm, D_MODEL), row),
                pl.BlockSpec((tm, D_MODEL), row),
                _resident((D_MODEL, D_MODEL), const),
                pl.BlockSpec((1, D_MODEL), const),
                _resident((D_MODEL, 2 * D_FF), const),
                pl.BlockSpec((CONV_W, D_FF), const),
                pl.BlockSpec((1, D_FF), const),
                _resident((D_FF, D_MODEL), const)]
    args = [x, o, wo, g.reshape(1, D_MODEL), w_in, cw, cb.reshape(1, D_FF), wd]
    scratch = [pltpu.VMEM((tm, D_MODEL), BF16), pltpu.VMEM((tm, D_MODEL), F32)]
    if sample:
        nseq = tm // DEC_SEQ
        state_spec = pl.BlockSpec((nseq, CONV_W - 1, D_FF), lambda i: (i, 0, 0))
        in_specs.append(state_spec)
        args.append(prev)
        conv_spec = state_spec
        conv_shape = jax.ShapeDtypeStruct((t // DEC_SEQ, CONV_W - 1, D_FF), F32)
    else:
        conv_spec = pl.BlockSpec((None, SUBLANES, D_FF), lambda i: (i // tiles_per_seq, 0, 0))
        conv_shape = jax.ShapeDtypeStruct((t // seq, SUBLANES, D_FF), F32)
        scratch.append(pltpu.VMEM((SUBLANES, D_FF), F32))
    if final:
        in_specs.append(pl.BlockSpec((1, D_MODEL), const))
        args.append(g_final.reshape(1, D_MODEL))
    return pl.pallas_call(
        functools.partial(_ffn_body, sample=sample, tiles_per_seq=tiles_per_seq, final=final),
        grid=(t // tm,),
        in_specs=in_specs,
        out_specs=[pl.BlockSpec((tm, D_MODEL), row), conv_spec],
        out_shape=[jax.ShapeDtypeStruct((t, D_MODEL), F32), conv_shape],
        scratch_shapes=scratch,
        compiler_params=_params(1), name=name,
    )(*args)


def _positions_last(x):
    return jnp.moveaxis(x, -3, -1)


def _positions_first(x):
    return jnp.moveaxis(x, -1, -3)


def kernel(x_prompt, x_sample, cache_swa_k, cache_swa_v, cache_sb_k, cache_sb_v, state_conv, page_table,
           norm_mix, norm_ffn, norm_final, w_qkv_a, sinks_a, w_o_a, w_qkv_b, bias_b, w_o_b,
           w_in_ffn, conv_w, conv_b, w_down):
    batch, seq, _ = x_prompt.shape
    dec_batch, dec_seq, _ = x_sample.shape
    depth = norm_mix.shape[0]
    past_len = page_table.shape[1] * cache_sb_k.shape[2]
    assert dec_seq == DEC_SEQ and cache_swa_k.shape[2] == WINDOW and cache_sb_k.shape[2] == PAGE_SIZE
    xp = x_prompt.reshape(batch * seq, D_MODEL)
    xs = x_sample.reshape(dec_batch * dec_seq, D_MODEL)
    rope_p = _rope_tables(jnp.arange(seq))
    rope_s = _rope_tables(past_len + jnp.arange(min(TOKEN_TILE, dec_batch * dec_seq)) % dec_seq)[:2]
    u = _suffix_matrix()
    swa_kc, swa_vc = _positions_last(cache_swa_k), _positions_last(cache_swa_v)
    sb_kc, sb_vc = _positions_last(cache_sb_k), _positions_last(cache_sb_v)

    swa_kp, swa_vp, swa_ks, swa_vs = [], [], [], []
    sb_kp, sb_vp, sb_ks, sb_vs = [], [], [], []
    conv_p, conv_s = [], []
    for i in range(depth):
        j = i // 2
        swa = i % 2 == 0
        w_qkv = (w_qkv_a if swa else w_qkv_b)[j]
        w_o = (w_o_a if swa else w_o_b)[j].astype(BF16)
        wqx = _expand_q_weight(w_qkv[:, :D_MODEL], ATTN_SCALE if swa else -ATTN_SCALE)
        wkv = w_qkv[:, D_MODEL:].astype(BF16)
        wk_t, wv_t = wkv[:, :D_KV].T, wkv[:, D_KV:].T
        qp, kp, vp, *blocks = _qkv_prompt_call(xp, norm_mix[i], wqx, wk_t, wv_t, rope_p if swa else None,
                                               batch, seq, not swa, f"qkv_prompt_{i}")
        qs, ks, vs = _qkv_sample_call(xs, norm_mix[i], wqx, wkv, rope_s if swa else None, f"qkv_sample_{i}")
        if swa:
            op = _swa_prompt_call(sinks_a[j], qp, kp, vp)
            os_, kw, vw = _swa_sample_call(sinks_a[j], qs, ks, vs, swa_kc, swa_vc, j)
            swa_kp.append(_positions_first(kp[..., seq - WINDOW:]))
            swa_vp.append(_positions_first(vp[..., seq - WINDOW:]))
            swa_ks.append(_positions_first(kw))
            swa_vs.append(_positions_first(vw))
        else:
            op = _sb_prompt_call(-bias_b[j], qp, blocks[0], blocks[1], u)
            os_ = _sb_sample_call(page_table, -bias_b[j], qs, ks, vs, u, sb_kc, sb_vc, j)
            sb_kp.append(_positions_first(kp))
            sb_vp.append(_positions_first(vp))
            sb_ks.append(ks.reshape(dec_batch, dec_seq, N_KV_HEADS, HEAD_DIM))
            sb_vs.append(vs.reshape(dec_batch, dec_seq, N_KV_HEADS, HEAD_DIM))
        g_final = norm_final if i == depth - 1 else None
        ffn_w = (w_o, norm_ffn[i], w_in_ffn[i].astype(BF16), conv_w[i], conv_b[i], w_down[i].astype(BF16))
        xp, cp = _ffn_call(xp, op, *ffn_w, None, g_final, seq, f"ffn_prompt_{i}")
        xs, cs = _ffn_call(xs, os_, *ffn_w, state_conv[i], g_final, None, f"ffn_sample_{i}")
        conv_p.append(cp[:, SUBLANES - (CONV_W - 1):])
        conv_s.append(cs)
    return (xp.reshape(batch, seq, D_MODEL), xs.reshape(dec_batch, dec_seq, D_MODEL),
            jnp.stack(swa_kp), jnp.stack(swa_vp), jnp.stack(sb_kp), jnp.stack(sb_vp), jnp.stack(conv_p),
            jnp.stack(swa_ks), jnp.stack(swa_vs), jnp.stack(sb_ks), jnp.stack(sb_vs), jnp.stack(conv_s))
```

```python
import functools

import jax
import jax.numpy as jnp
from jax import lax
from jax.experimental import pallas as pl
from jax.experimental.pallas import tpu as pltpu

F32 = jnp.float32
BF16 = jnp.bfloat16

D_MODEL = 1024
HEAD_DIM = 64
N_HEADS = 16
N_KV_HEADS = 4
GROUP = N_HEADS // N_KV_HEADS
D_KV = N_KV_HEADS * HEAD_DIM
DEC_SEQ = 8
WINDOW = 128
PAGE_SIZE = 128
ROPE_THETA = 10000.0
D_FF = 2816
CONV_W = 3
RMS_EPS = 1e-6
ATTN_SCALE = HEAD_DIM ** -0.5

LANES = 128
SUBLANES = 8
D_QX = N_HEADS * LANES
FF_CHUNK = 256
N_FF_CHUNKS = D_FF // FF_CHUNK
TOKEN_TILE = 512
FFN_TILE = 512
SB_BLOCK = 2 * LANES
PAGES_PER_STEP = 16
PAGE_BUFFERS = 3
VMEM_LIMIT = 56 * 1024 * 1024

assert D_FF % FF_CHUNK == 0 and TOKEN_TILE % SB_BLOCK == 0
assert DEC_SEQ == SUBLANES and WINDOW == LANES and PAGE_SIZE == LANES and 2 * HEAD_DIM == LANES


def _params(n_axes):
    return pltpu.CompilerParams(dimension_semantics=("arbitrary",) * n_axes,
                                vmem_limit_bytes=VMEM_LIMIT)


def _resident(shape, index_map):
    return pl.BlockSpec(shape, index_map, pipeline_mode=pl.Buffered(1))


def _rmsnorm(x, g):
    r = lax.rsqrt(jnp.mean(x * x, axis=-1, keepdims=True) + RMS_EPS)
    return x * r * g


def _dot(a, b):
    return jnp.dot(a, b, preferred_element_type=F32)


def _dot_nt(a, b):
    return lax.dot_general(a, b, (((1,), (1,)), ((), ())), preferred_element_type=F32)


def _kv_pair(kv):
    return kv // 2, kv % 2


def _rope_tile(x, cosf, sinf):
    lane = lax.broadcasted_iota(jnp.int32, x.shape, 1)
    first = (lane % HEAD_DIM) < (HEAD_DIM // 2)
    other = jnp.where(first, pltpu.roll(x, LANES - HEAD_DIM // 2, axis=1),
                      pltpu.roll(x, HEAD_DIM // 2, axis=1))
    return x * cosf + other * sinf


def _rope_rows(x, cos_t, sin_t):
    half = HEAD_DIM // 2
    out = []
    for kv in range(N_KV_HEADS):
        x1 = x[kv * HEAD_DIM:kv * HEAD_DIM + half]
        x2 = x[kv * HEAD_DIM + half:(kv + 1) * HEAD_DIM]
        out += [x1 * cos_t - x2 * sin_t, x2 * cos_t + x1 * sin_t]
    return jnp.concatenate(out, axis=0)


def _qkv_prompt_body(*refs, rope, blocked):
    refs = list(refs)
    x_ref, g_ref, wq_ref, wk_ref, wv_ref = refs[:5]
    refs = refs[5:]
    if rope:
        cos_ref, sin_ref, cos_t_ref, sin_t_ref = refs[:4]
        refs = refs[4:]
    q_ref, k_ref, v_ref = refs[:3]
    h = _rmsnorm(x_ref[...], g_ref[...]).astype(BF16)
    tm = h.shape[0]
    q = _dot(h, wq_ref[...])
    k_t = _dot_nt(wk_ref[...], h)
    v_t = _dot_nt(wv_ref[...], h)
    if rope:
        cosf, sinf = cos_ref[...], sin_ref[...]
        for j in range(N_HEADS):
            sl = slice(j * LANES, (j + 1) * LANES)
            q_ref[:, sl] = _rope_tile(q[:, sl], cosf, sinf).astype(q_ref.dtype)
        k_t = _rope_rows(k_t, cos_t_ref[...], sin_t_ref[...])
    else:
        q_ref[...] = q.astype(q_ref.dtype)
    k_ref[...] = k_t.reshape(N_KV_HEADS, HEAD_DIM, tm)
    v_ref[...] = v_t.reshape(N_KV_HEADS, HEAD_DIM, tm)
    if blocked:
        kb_ref, vb_ref = refs[3:5]
        for c in range(tm // SB_BLOCK):
            sl = slice(c * SB_BLOCK, (c + 1) * SB_BLOCK)
            kb_ref[c] = k_t[:, sl].reshape(N_KV_HEADS, HEAD_DIM, SB_BLOCK)
            vb_ref[c] = v_t[:, sl].reshape(N_KV_HEADS, HEAD_DIM, SB_BLOCK)


def _qkv_prompt_call(x, g, wqx, wk_t, wv_t, tables, batch, seq, blocked, name):
    tm = TOKEN_TILE
    tps = seq // tm
    rope = tables is not None
    row = lambda i: (i, 0)
    const = lambda i: (0, 0)
    in_specs = [pl.BlockSpec((tm, D_MODEL), row),
                pl.BlockSpec((1, D_MODEL), const),
                _resident((D_MODEL, D_QX), const),
                _resident((D_KV, D_MODEL), const),
                _resident((D_KV, D_MODEL), const)]
    args = [x, g.reshape(1, D_MODEL), wqx, wk_t, wv_t]
    if rope:
        cosf, sinf, cos_t, sin_t = tables
        in_specs += [pl.BlockSpec((tm, LANES), lambda i: (i % tps, 0))] * 2
        in_specs += [pl.BlockSpec((HEAD_DIM // 2, tm), lambda i: (0, i % tps))] * 2
        args += [cosf, sinf, cos_t, sin_t]
    kv_spec = pl.BlockSpec((None, N_KV_HEADS, HEAD_DIM, tm), lambda i: (i // tps, 0, 0, i % tps))
    kv_shape = jax.ShapeDtypeStruct((batch, N_KV_HEADS, HEAD_DIM, seq), F32)
    out_specs = [pl.BlockSpec((tm, D_QX), row), kv_spec, kv_spec]
    out_shape = [jax.ShapeDtypeStruct((batch * seq, D_QX), BF16), kv_shape, kv_shape]
    if blocked:
        nb = tm // SB_BLOCK
        b_spec = pl.BlockSpec((None, nb, N_KV_HEADS, HEAD_DIM, SB_BLOCK), lambda i: (i // tps, i % tps, 0, 0, 0))
        b_shape = jax.ShapeDtypeStruct((batch, seq // SB_BLOCK, N_KV_HEADS, HEAD_DIM, SB_BLOCK), F32)
        out_specs += [b_spec, b_spec]
        out_shape += [b_shape, b_shape]
    return pl.pallas_call(
        functools.partial(_qkv_prompt_body, rope=rope, blocked=blocked),
        grid=(batch * seq // tm,),
        in_specs=in_specs, out_specs=out_specs, out_shape=out_shape,
        compiler_params=_params(1), name=name,
    )(*args)


def _qkv_sample_body(*refs, rope):
    if rope:
        x_ref, g_ref, wq_ref, wkv_ref, cos_ref, sin_ref, q_ref, k_ref, v_ref = refs
    else:
        x_ref, g_ref, wq_ref, wkv_ref, q_ref, k_ref, v_ref = refs
    h = _rmsnorm(x_ref[...], g_ref[...]).astype(BF16)
    q = _dot(h, wq_ref[...])
    kv = _dot(h, wkv_ref[...])
    if rope:
        cosf, sinf = cos_ref[...], sin_ref[...]
        for j in range(N_HEADS):
            sl = slice(j * LANES, (j + 1) * LANES)
            q_ref[:, sl] = _rope_tile(q[:, sl], cosf, sinf)
        for j in range(D_KV // LANES):
            sl = slice(j * LANES, (j + 1) * LANES)
            k_ref[:, sl] = _rope_tile(kv[:, sl], cosf, sinf)
    else:
        q_ref[...] = q
        k_ref[...] = kv[:, :D_KV]
    v_ref[...] = kv[:, D_KV:]


def _qkv_sample_call(x, g, wqx, wkv, tables, name):
    t = x.shape[0]
    tm = min(TOKEN_TILE, t)
    rope = tables is not None
    row = lambda i: (i, 0)
    const = lambda i: (0, 0)
    in_specs = [pl.BlockSpec((tm, D_MODEL), row),
                pl.BlockSpec((1, D_MODEL), const),
                _resident((D_MODEL, D_QX), const),
                _resident((D_MODEL, 2 * D_KV), const)]
    args = [x, g.reshape(1, D_MODEL), wqx, wkv]
    if rope:
        in_specs += [pl.BlockSpec((tm, LANES), const)] * 2
        args += list(tables)
    return pl.pallas_call(
        functools.partial(_qkv_sample_body, rope=rope),
        grid=(t // tm,),
        in_specs=in_specs,
        out_specs=[pl.BlockSpec((tm, D_QX), row), pl.BlockSpec((tm, D_KV), row), pl.BlockSpec((tm, D_KV), row)],
        out_shape=[jax.ShapeDtypeStruct((t, D_QX), F32),
                   jax.ShapeDtypeStruct((t, D_KV), F32),
                   jax.ShapeDtypeStruct((t, D_KV), F32)],
        compiler_params=_params(1), name=name,
    )(*args)


def _rope_tables(pos):
    inv_freq = ROPE_THETA ** (-jnp.arange(0, HEAD_DIM, 2, dtype=F32) / HEAD_DIM)
    ang = pos.astype(F32)[:, None] * inv_freq[None, :]
    cos, sin = jnp.cos(ang), jnp.sin(ang)
    cosf = jnp.concatenate([cos, cos, cos, cos], axis=1)
    sinf = jnp.concatenate([-sin, sin, -sin, sin], axis=1)
    return cosf, sinf, cos.T, sin.T


def _expand_q_weight(w_q, scale):
    w = (w_q * scale).reshape(D_MODEL, N_HEADS, 1, HEAD_DIM)
    half = ((jnp.arange(N_HEADS) // GROUP) % 2)[None, :, None, None]
    w = jnp.where(half == jnp.arange(2)[None, None, :, None], w, 0.0)
    return w.reshape(D_MODEL, D_QX).astype(BF16)


def _half_mask(shape, half):
    lane = lax.broadcasted_iota(jnp.int32, shape, 1)
    return (lane >= HEAD_DIM) if half else (lane < HEAD_DIM)


def _merge_heads(even, odd, even_half, odd_half):
    if even_half != 0:
        even = pltpu.roll(even, HEAD_DIM, axis=1)
    if odd_half != 1:
        odd = pltpu.roll(odd, HEAD_DIM, axis=1)
    return jnp.where(_half_mask(even.shape, 0), even, odd)


def _merge_all_heads(parts):
    tiles = [_merge_heads(parts[2 * p][0], parts[2 * p + 1][0], parts[2 * p][1], parts[2 * p + 1][1])
             for p in range(N_HEADS // 2)]
    return jnp.concatenate(tiles, axis=1)


def _block_diag_q(qx):
    zeros = jnp.zeros((SUBLANES, LANES), qx.dtype)
    rows = []
    for h in range(N_HEADS):
        pair, _ = _kv_pair(h // GROUP)
        tile = qx[:, h * LANES:(h + 1) * LANES]
        rows.append(jnp.concatenate([tile, zeros] if pair == 0 else [zeros, tile], axis=1))
    return jnp.concatenate(rows, axis=0)


def _unpack_block_diag(acc):
    parts = []
    for h in range(N_HEADS):
        pair, half = _kv_pair(h // GROUP)
        parts.append((acc[h * SUBLANES:(h + 1) * SUBLANES, pair * LANES:(pair + 1) * LANES], half))
    return _merge_all_heads(parts)


def _per_row_heads(scalar_ref, rows_per_head, width):
    return jnp.concatenate([jnp.full((rows_per_head, width), scalar_ref[h], F32) for h in range(N_HEADS)], axis=0)


def _pad_rows(pad_scr, x):
    pad_scr[...] = jnp.zeros(pad_scr.shape, F32)
    pad_scr[0:DEC_SEQ, :] = x
    return pad_scr[...]


def _swa_prompt_body(sink_ref, q_ref, kp_ref, kc_ref, vp_ref, vc_ref, o_ref):
    j = pl.program_id(1)
    rows = GROUP * WINDOW
    a = lax.broadcasted_iota(jnp.int32, (rows, 2 * WINDOW), 0) % WINDOW
    c = lax.broadcasted_iota(jnp.int32, (rows, 2 * WINDOW), 1)
    mask = (c >= a) & (c <= a + WINDOW) & ((j > 0) | (c >= WINDOW))
    kvs = range(N_KV_HEADS)

    def band(prev, cur, kv):
        rows2 = slice(2 * (kv // 2), 2 * (kv // 2) + 2)
        return jnp.concatenate([prev[rows2].reshape(LANES, WINDOW), cur[rows2].reshape(LANES, WINDOW)],
                               axis=1).astype(BF16)

    zs = []
    for kv in kvs:
        q4 = jnp.concatenate([q_ref[:, h * LANES:(h + 1) * LANES] for h in range(kv * GROUP, (kv + 1) * GROUP)], axis=0)
        zs.append(jnp.where(mask, _dot(q4, band(kp_ref, kc_ref, kv)), -jnp.inf))
    ps, denoms = [], []
    for kv in kvs:
        sink = jnp.concatenate([jnp.full((WINDOW, LANES), sink_ref[h], F32)
                                for h in range(kv * GROUP, (kv + 1) * GROUP)], axis=0)
        m = jnp.maximum(jnp.max(zs[kv], axis=-1, keepdims=True), sink)
        p = jnp.exp(zs[kv] - jnp.concatenate([m, m], axis=1)).astype(BF16)
        denoms.append(_dot(p, jnp.ones((2 * WINDOW, LANES), BF16)) + jnp.exp(sink - m))
        ps.append(p)
    parts = []
    for kv in kvs:
        o4 = _dot_nt(ps[kv], band(vp_ref, vc_ref, kv)) / denoms[kv]
        parts += [(o4[g * WINDOW:(g + 1) * WINDOW], _kv_pair(kv)[1]) for g in range(GROUP)]
    o_ref[...] = _merge_all_heads(parts).astype(o_ref.dtype)


def _swa_prompt_call(sink, qx, k_t, v_t):
    batch, _, _, seq = k_t.shape
    nb = seq // WINDOW
    kv_spec = lambda f: pl.BlockSpec((None, N_KV_HEADS, HEAD_DIM, WINDOW), f)
    cur = lambda b, j: (b, 0, 0, j)
    prev = lambda b, j: (b, 0, 0, jnp.maximum(j - 1, 0))
    return pl.pallas_call(
        _swa_prompt_body,
        grid=(batch, nb),
        in_specs=[pl.BlockSpec(memory_space=pltpu.SMEM),
                  pl.BlockSpec((WINDOW, D_QX), lambda b, j: (b * nb + j, 0)),
                  kv_spec(prev), kv_spec(cur), kv_spec(prev), kv_spec(cur)],
        out_specs=pl.BlockSpec((WINDOW, D_MODEL), lambda b, j: (b * nb + j, 0)),
        out_shape=jax.ShapeDtypeStruct((batch * seq, D_MODEL), BF16),
        compiler_params=_params(2), name="swa_prompt",
    )(sink, qx, k_t, k_t, v_t, v_t)


def _swa_sample_body(sink_ref, q_ref, kn_ref, vn_ref, kc_ref, vc_ref, o_ref, kw_ref, vw_ref, pad_scr):
    rows = N_HEADS * DEC_SEQ
    t = lax.broadcasted_iota(jnp.int32, (rows, WINDOW), 0) % DEC_SEQ
    c = lax.broadcasted_iota(jnp.int32, (rows, WINDOW), 1)
    kc_t = kc_ref[...].reshape(D_KV, WINDOW)
    vc_t = vc_ref[...].reshape(D_KV, WINDOW)
    kn = _pad_rows(pad_scr, kn_ref[...])
    vn = _pad_rows(pad_scr, vn_ref[...])

    lane = lax.broadcasted_iota(jnp.int32, (D_KV, WINDOW), 1)
    keep = WINDOW - DEC_SEQ
    shift = lambda x: pltpu.roll(x, keep, axis=1)
    kw_ref[...] = jnp.where(lane < keep, shift(kc_t), shift(kn.T)).reshape(N_KV_HEADS, HEAD_DIM, WINDOW)
    vw_ref[...] = jnp.where(lane < keep, shift(vc_t), shift(vn.T)).reshape(N_KV_HEADS, HEAD_DIM, WINDOW)

    qbd = _block_diag_q(q_ref[...]).astype(BF16)
    sink = _per_row_heads(sink_ref, DEC_SEQ, LANES)
    zc = jnp.where(c >= t, _dot(qbd, kc_t.astype(BF16)), -jnp.inf)
    zn = jnp.where(c <= t, _dot_nt(qbd, kn.astype(BF16)), -jnp.inf)
    m = jnp.maximum(jnp.maximum(jnp.max(zc, axis=-1, keepdims=True), jnp.max(zn, axis=-1, keepdims=True)), sink)
    pc = jnp.exp(zc - m).astype(BF16)
    pn = jnp.exp(zn - m).astype(BF16)
    ones = jnp.ones((WINDOW, LANES), BF16)
    denom = _dot(pc, ones) + _dot(pn, ones) + jnp.exp(sink - m)
    acc = _dot_nt(pc, vc_t.astype(BF16)) + _dot(pn, vn.astype(BF16))
    o_ref[...] = _unpack_block_diag(acc / jnp.concatenate([denom, denom], axis=1))


def _swa_sample_call(sink, qx, k_new, v_new, k_cache_t, v_cache_t, layer):
    n = k_cache_t.shape[1]
    seq_rows = lambda s: (s, 0)
    cache_in = pl.BlockSpec((None, None, N_KV_HEADS, HEAD_DIM, WINDOW), lambda s: (layer, s, 0, 0, 0))
    cache_out = pl.BlockSpec((None, N_KV_HEADS, HEAD_DIM, WINDOW), lambda s: (s, 0, 0, 0))
    win_shape = jax.ShapeDtypeStruct((n, N_KV_HEADS, HEAD_DIM, WINDOW), F32)
    return pl.pallas_call(
        _swa_sample_body,
        grid=(n,),
        in_specs=[pl.BlockSpec(memory_space=pltpu.SMEM),
                  pl.BlockSpec((DEC_SEQ, D_QX), seq_rows),
                  pl.BlockSpec((DEC_SEQ, D_KV), seq_rows),
                  pl.BlockSpec((DEC_SEQ, D_KV), seq_rows),
                  cache_in, cache_in],
        out_specs=[pl.BlockSpec((DEC_SEQ, D_MODEL), seq_rows), cache_out, cache_out],
        out_shape=[jax.ShapeDtypeStruct((n * DEC_SEQ, D_MODEL), F32), win_shape, win_shape],
        scratch_shapes=[pltpu.VMEM((WINDOW, D_KV), F32)],
        compiler_params=_params(1), name="swa_sample",
    )(sink, qx, k_new, v_new, k_cache_t, v_cache_t)


def _suffix_matrix():
    j = jnp.arange(2 * LANES)[:, None] % LANES
    s = jnp.arange(2 * LANES)[None, :]
    return ((s >= LANES) | (j >= s)).astype(BF16)


def _hi_lo(x):
    hi = x.astype(BF16)
    lo = (x - hi.astype(F32)).astype(BF16)
    return jnp.concatenate([hi, lo], axis=1)


def _sb_log_rest(neg_z_blocks, masks):
    rows = []
    for zn, mk in zip(neg_z_blocks, masks):
        lr = jnp.minimum(zn, 0.0) - jnp.log(1.0 + jnp.exp(-jnp.abs(zn)))
        rows.append(_hi_lo(lr if mk is None else jnp.where(mk, lr, 0.0)))
    return jnp.concatenate(rows, axis=0)


def _sb_weights(neg_z_blocks, masks, sums, carry):
    m = neg_z_blocks[0].shape[0]
    weights = []
    for n, (zn, mk) in enumerate(zip(neg_z_blocks, masks)):
        s = sums[n * m:(n + 1) * m]
        a = jnp.exp(s[:, :LANES] + carry - zn)
        weights.append(a if mk is None else jnp.where(mk, a, 0.0))
        carry = carry + s[:, LANES:]
    return weights, carry


def _sb_blocks(neg_z_blocks, masks, u, carry):
    sums = _dot(_sb_log_rest(neg_z_blocks, masks), u)
    return _sb_weights(neg_z_blocks, masks, sums, carry)


def _sb_prompt_body(nbias_ref, q_ref, k_ref, v_ref, u_ref, o_ref, carry_scr, acc_scr):
    i = pl.program_id(1)
    rows = GROUP * SB_BLOCK
    sub = SB_BLOCK // LANES
    u = u_ref[...]
    t = lax.broadcasted_iota(jnp.int32, (rows, LANES), 0) % SB_BLOCK
    s = lax.broadcasted_iota(jnp.int32, (rows, LANES), 1)
    diag_masks = [(s + n * LANES) < t for n in reversed(range(sub))]

    def block(c, masks, first):
        kvs = range(N_KV_HEADS)
        tile = lambda ref, kv: ref[c, 2 * (kv // 2):2 * (kv // 2) + 2].reshape(LANES, SB_BLOCK).astype(BF16)
        zs = []
        for kv in kvs:
            heads = range(kv * GROUP, (kv + 1) * GROUP)
            q4 = jnp.concatenate([q_ref[:, h * LANES:(h + 1) * LANES] for h in heads], axis=0)
            nbias = jnp.concatenate([jnp.full((SB_BLOCK, LANES), nbias_ref[h], F32) for h in heads], axis=0)
            zn = _dot(q4, tile(k_ref, kv))
            zs.append([zn[:, n * LANES:(n + 1) * LANES] + nbias for n in reversed(range(sub))])
        log_rest = [_sb_log_rest(zs[kv], masks) for kv in kvs]
        sums = [_dot(log_rest[kv], u) for kv in kvs]
        for kv in kvs:
            carry = jnp.zeros((rows, LANES), F32) if first else carry_scr[kv]
            weights, carry = _sb_weights(zs[kv], masks, sums[kv], carry)
            a = jnp.concatenate(list(reversed(weights)), axis=1).astype(BF16)
            pv = _dot_nt(a, tile(v_ref, kv))
            acc_scr[kv] = pv if first else acc_scr[kv] + pv
            carry_scr[kv] = carry

    block(i, diag_masks, True)
    lax.fori_loop(0, i, lambda n, _: (block(i - 1 - n, [None] * sub, False), 0)[1], 0)
    parts = []
    for kv in range(N_KV_HEADS):
        acc = acc_scr[kv]
        parts += [(acc[g * SB_BLOCK:(g + 1) * SB_BLOCK], _kv_pair(kv)[1]) for g in range(GROUP)]
    o_ref[...] = _merge_all_heads(parts).astype(o_ref.dtype)


def _sb_prompt_call(neg_bias, neg_qx, k_blocks, v_blocks, u):
    batch, nb = k_blocks.shape[:2]
    rows = GROUP * SB_BLOCK
    kv_spec = pl.BlockSpec((None, nb, N_KV_HEADS, HEAD_DIM, SB_BLOCK), lambda b, i: (b, 0, 0, 0, 0))
    return pl.pallas_call(
        _sb_prompt_body,
        grid=(batch, nb),
        in_specs=[pl.BlockSpec(memory_space=pltpu.SMEM),
                  pl.BlockSpec((SB_BLOCK, D_QX), lambda b, i: (b * nb + i, 0)),
                  kv_spec, kv_spec,
                  pl.BlockSpec((2 * LANES, 2 * LANES), lambda b, i: (0, 0))],
        out_specs=pl.BlockSpec((SB_BLOCK, D_MODEL), lambda b, i: (b * nb + i, 0)),
        out_shape=jax.ShapeDtypeStruct((batch * nb * SB_BLOCK, D_MODEL), BF16),
        scratch_shapes=[pltpu.VMEM((N_KV_HEADS, rows, LANES), F32),
                        pltpu.VMEM((N_KV_HEADS, rows, LANES), F32)],
        compiler_params=_params(2), name="sb_prompt",
    )(neg_bias, neg_qx, k_blocks, v_blocks, u)


def _page_copies(pt_ref, pool_k, pool_v, kbuf, vbuf, sems, layer, n_chunks, step, slot):
    n_pages = pt_ref.shape[1]
    seq, chunk = step // n_chunks, step % n_chunks
    copies = []
    for j in range(PAGES_PER_STEP):
        page = pt_ref[seq, n_pages - 1 - (chunk * PAGES_PER_STEP + j)]
        copies.append(pltpu.make_async_copy(pool_k.at[layer, page], kbuf.at[slot, j], sems.at[slot, 0]))
        copies.append(pltpu.make_async_copy(pool_v.at[layer, page], vbuf.at[slot, j], sems.at[slot, 1]))
    return copies


def _sb_sample_body(pt_ref, nbias_ref, q_ref, kn_ref, vn_ref, u_ref, pool_k, pool_v, o_ref,
                    qbd_scr, bias_scr, carry_scr, acc_scr, pad_scr, kbuf, vbuf, sems, *, layer, n_steps, n_chunks):
    c = pl.program_id(1)
    step = pl.program_id(0) * n_chunks + c
    slot = step % PAGE_BUFFERS
    rows = N_HEADS * DEC_SEQ
    u = u_ref[...]
    copies = functools.partial(_page_copies, pt_ref, pool_k, pool_v, kbuf, vbuf, sems, layer, n_chunks)

    @pl.when(step == 0)
    def _first_fetches():
        for d in range(PAGE_BUFFERS - 1):
            for cp in copies(d, d):
                cp.start()

    @pl.when(c == 0)
    def _new_tokens():
        qbd = _block_diag_q(q_ref[...]).astype(BF16)
        nbias = _per_row_heads(nbias_ref, DEC_SEQ, LANES)
        qbd_scr[...] = qbd
        bias_scr[...] = nbias
        kn = _pad_rows(pad_scr, kn_ref[...]).astype(BF16)
        vn = _pad_rows(pad_scr, vn_ref[...]).astype(BF16)
        t = lax.broadcasted_iota(jnp.int32, (rows, LANES), 0) % DEC_SEQ
        s = lax.broadcasted_iota(jnp.int32, (rows, LANES), 1)
        (a,), carry = _sb_blocks([_dot_nt(qbd, kn) + nbias], [s < t], u, jnp.zeros((rows, LANES), F32))
        carry_scr[...] = carry
        acc_scr[...] = _dot(a.astype(BF16), vn)

    ahead = step + (PAGE_BUFFERS - 1)
    for cp in copies(step, slot):
        cp.wait()
    for cp in copies(jnp.minimum(ahead, n_steps - 1), ahead % PAGE_BUFFERS):
        cp.start()

    qbd, nbias = qbd_scr[...], bias_scr[...]
    page = lambda buf, j: buf[slot, j].reshape(D_KV, PAGE_SIZE).astype(BF16)
    k_t = jnp.concatenate([page(kbuf, j) for j in range(PAGES_PER_STEP)], axis=1)
    v_t = jnp.concatenate([page(vbuf, j) for j in range(PAGES_PER_STEP)], axis=1)
    zn = _dot(qbd, k_t)
    zs = [zn[:, n * LANES:(n + 1) * LANES] + nbias for n in range(PAGES_PER_STEP)]
    weights, carry = _sb_blocks(zs, [None] * PAGES_PER_STEP, u, carry_scr[...])
    acc = acc_scr[...] + _dot_nt(jnp.concatenate(weights, axis=1).astype(BF16), v_t)
    carry_scr[...] = carry
    acc_scr[...] = acc

    @pl.when(c == n_chunks - 1)
    def _sequence_done():
        o_ref[...] = _unpack_block_diag(acc)

    @pl.when(step == n_steps - 1)
    def _drain():
        for d in range(1, PAGE_BUFFERS):
            for cp in copies(step, (step + d) % PAGE_BUFFERS):
                cp.wait()


def _sb_sample_call(page_table, neg_bias, neg_qx, k_new, v_new, u, pool_k_t, pool_v_t, layer):
    n, n_pages = page_table.shape
    assert n_pages % PAGES_PER_STEP == 0
    seq_rows = lambda s, c, pt: (s, 0)
    n_chunks = n_pages // PAGES_PER_STEP
    assert n * n_chunks >= PAGE_BUFFERS
    page_buf = pltpu.VMEM((PAGE_BUFFERS, PAGES_PER_STEP, N_KV_HEADS, HEAD_DIM, PAGE_SIZE), F32)
    rows = N_HEADS * DEC_SEQ
    grid_spec = pltpu.PrefetchScalarGridSpec(
        num_scalar_prefetch=1,
        grid=(n, n_chunks),
        in_specs=[pl.BlockSpec(memory_space=pltpu.SMEM),
                  pl.BlockSpec((DEC_SEQ, D_QX), seq_rows),
                  pl.BlockSpec((DEC_SEQ, D_KV), seq_rows),
                  pl.BlockSpec((DEC_SEQ, D_KV), seq_rows),
                  pl.BlockSpec((2 * LANES, 2 * LANES), lambda s, c, pt: (0, 0)),
                  pl.BlockSpec(memory_space=pl.ANY),
                  pl.BlockSpec(memory_space=pl.ANY)],
        out_specs=pl.BlockSpec((DEC_SEQ, D_MODEL), seq_rows),
        scratch_shapes=[pltpu.VMEM((rows, D_KV), BF16),
                        pltpu.VMEM((rows, LANES), F32),
                        pltpu.VMEM((rows, LANES), F32),
                        pltpu.VMEM((rows, D_KV), F32),
                        pltpu.VMEM((PAGE_SIZE, D_KV), F32),
                        page_buf, page_buf,
                        pltpu.SemaphoreType.DMA((PAGE_BUFFERS, 2))],
    )
    return pl.pallas_call(
        functools.partial(_sb_sample_body, layer=layer, n_steps=n * n_chunks, n_chunks=n_chunks),
        grid_spec=grid_spec,
        out_shape=jax.ShapeDtypeStruct((n * DEC_SEQ, D_MODEL), F32),
        compiler_params=_params(2), name="sb_sample",
    )(page_table, neg_bias, neg_qx, k_new, v_new, u, pool_k_t, pool_v_t)


def _ffn_body(*refs, sample, tiles_per_seq, final):
    refs = list(refs)
    x_ref, o_ref, wo_ref, g_ref, win_ref, cw_ref, cb_ref, wd_ref = refs[:8]
    refs = refs[8:]
    prev_ref = refs.pop(0) if sample else None
    gf_ref = refs.pop(0) if final else None
    xo_ref, conv_ref, h_scr, acc_scr = refs[:4]
    carry_scr = None if sample else refs[4]
    i = pl.program_id(0)
    tm = x_ref.shape[0]
    x1 = x_ref[...] + _dot(o_ref[...].astype(BF16), wo_ref[...])
    h_scr[...] = _rmsnorm(x1, g_ref[...]).astype(BF16)
    acc_scr[...] = x1
    if not sample:
        @pl.when(i % tiles_per_seq == 0)
        def _sequence_start():
            carry_scr[...] = jnp.zeros(carry_scr.shape, F32)

    def activation(c):
        sl = slice(c * FF_CHUNK, (c + 1) * FF_CHUNK)
        h = h_scr[...]
        gate = _dot(h, win_ref[:, sl])
        up = _dot(h, win_ref[:, D_FF + c * FF_CHUNK:D_FF + (c + 1) * FF_CHUNK])
        cw = cw_ref[:, sl]
        if sample:
            nseq = tm // DEC_SEQ
            prev = prev_ref[:, :, sl]
            expand = lambda p: jnp.broadcast_to(p, (nseq, DEC_SEQ, FF_CHUNK)).reshape(tm, FF_CHUNK)
            p0, p1 = expand(prev[:, 0:1, :]), expand(prev[:, 1:2, :])
            t = lax.broadcasted_iota(jnp.int32, (tm, FF_CHUNK), 0) % DEC_SEQ
            g1 = jnp.where(t == 0, p1, pltpu.roll(gate, 1, axis=0))
            g2 = jnp.where(t == 0, p0, jnp.where(t == 1, p1, pltpu.roll(gate, 2, axis=0)))
            conv_ref[:, :, sl] = gate.reshape(nseq, DEC_SEQ, FF_CHUNK)[:, DEC_SEQ - (CONV_W - 1):, :]
        else:
            prev8 = carry_scr[:, sl]
            top = gate[0:SUBLANES]
            r8 = lax.broadcasted_iota(jnp.int32, (SUBLANES, FF_CHUNK), 0)
            top1 = jnp.where(r8 < 1, pltpu.roll(prev8, 1, axis=0), pltpu.roll(top, 1, axis=0))
            top2 = jnp.where(r8 < 2, pltpu.roll(prev8, 2, axis=0), pltpu.roll(top, 2, axis=0))
            g1 = jnp.concatenate([top1, pltpu.roll(gate, 1, axis=0)[SUBLANES:]], axis=0)
            g2 = jnp.concatenate([top2, pltpu.roll(gate, 2, axis=0)[SUBLANES:]], axis=0)
            last8 = gate[tm - SUBLANES:]
            carry_scr[:, sl] = last8
            conv_ref[:, sl] = last8
        gc = cb_ref[:, sl] + cw[0:1] * g2 + cw[1:2] * g1 + cw[2:3] * gate
        return (gc / (1.0 + jnp.exp(-gc)) * up).astype(BF16)

    act = activation(0)
    for c in range(1, N_FF_CHUNKS):
        nxt = activation(c)
        acc_scr[...] += _dot(act, wd_ref[(c - 1) * FF_CHUNK:c * FF_CHUNK, :])
        act = nxt
    x2 = acc_scr[...] + _dot(act, wd_ref[(N_FF_CHUNKS - 1) * FF_CHUNK:, :])
    xo_ref[...] = _rmsnorm(x2, gf_ref[...]) if final else x2


def _ffn_call(x, o, wo, g, w_in, cw, cb, wd, prev, g_final, seq, name):
    t = x.shape[0]
    sample = prev is not None
    final = g_final is not None
    tm = min(FFN_TILE, t)
    tiles_per_seq = None if sample else seq // tm
    row = lambda i: (i, 0)
    const = lambda i: (0, 0)
    in_specs = [pl.BlockSpec((tm, D_MODEL), row),
                pl.BlockSpec((tm, D_MODEL), row),
                _resident((D_MODEL, D_MODEL), const),
                pl.BlockSpec((1, D_MODEL), const),
                _resident((D_MODEL, 2 * D_FF), const),
                pl.BlockSpec((CONV_W, D_FF), const),
                pl.BlockSpec((1, D_FF), const),
                _resident((D_FF, D_MODEL), const)]
    args = [x, o, wo, g.reshape(1, D_MODEL), w_in, cw, cb.reshape(1, D_FF), wd]
    scratch = [pltpu.VMEM((tm, D_MODEL), BF16), pltpu.VMEM((tm, D_MODEL), F32)]
    if sample:
        nseq = tm // DEC_SEQ
        state_spec = pl.BlockSpec((nseq, CONV_W - 1, D_FF), lambda i: (i, 0, 0))
        in_specs.append(state_spec)
        args.append(prev)
        conv_spec = state_spec
        conv_shape = jax.ShapeDtypeStruct((t // DEC_SEQ, CONV_W - 1, D_FF), F32)
    else:
        conv_spec = pl.BlockSpec((None, SUBLANES, D_FF), lambda i: (i // tiles_per_seq, 0, 0))
        conv_shape = jax.ShapeDtypeStruct((t // seq, SUBLANES, D_FF), F32)
        scratch.append(pltpu.VMEM((SUBLANES, D_FF), F32))
    if final:
        in_specs.append(pl.BlockSpec((1, D_MODEL), const))
        args.append(g_final.reshape(1, D_MODEL))
    return pl.pallas_call(
        functools.partial(_ffn_body, sample=sample, tiles_per_seq=tiles_per_seq, final=final),
        grid=(t // tm,),
        in_specs=in_specs,
        out_specs=[pl.BlockSpec((tm, D_MODEL), row), conv_spec],
        out_shape=[jax.ShapeDtypeStruct((t, D_MODEL), F32), conv_shape],
        scratch_shapes=scratch,
        compiler_params=_params(1), name=name,
    )(*args)


def _positions_last(x):
    return jnp.moveaxis(x, -3, -1)


def _positions_first(x):
    return jnp.moveaxis(x, -1, -3)


def kernel(x_prompt, x_sample, cache_swa_k, cache_swa_v, cache_sb_k, cache_sb_v, state_conv, page_table,
           norm_mix, norm_ffn, norm_final, w_qkv_a, sinks_a, w_o_a, w_qkv_b, bias_b, w_o_b,
           w_in_ffn, conv_w, conv_b, w_down):
    batch, seq, _ = x_prompt.shape
    dec_batch, dec_seq, _ = x_sample.shape
    depth = norm_mix.shape[0]
    past_len = page_table.shape[1] * cache_sb_k.shape[2]
    assert dec_seq == DEC_SEQ and cache_swa_k.shape[2] == WINDOW and cache_sb_k.shape[2] == PAGE_SIZE
    xp = x_prompt.reshape(batch * seq, D_MODEL)
    xs = x_sample.reshape(dec_batch * dec_seq, D_MODEL)
    rope_p = _rope_tables(jnp.arange(seq))
    rope_s = _rope_tables(past_len + jnp.arange(min(TOKEN_TILE, dec_batch * dec_seq)) % dec_seq)[:2]
    u = _suffix_matrix()
    swa_kc, swa_vc = _positions_last(cache_swa_k), _positions_last(cache_swa_v)
    sb_kc, sb_vc = _positions_last(cache_sb_k), _positions_last(cache_sb_v)

    swa_kp, swa_vp, swa_ks, swa_vs = [], [], [], []
    sb_kp, sb_vp, sb_ks, sb_vs = [], [], [], []
    conv_p, conv_s = [], []
    for i in range(depth):
        j = i // 2
        swa = i % 2 == 0
        w_qkv = (w_qkv_a if swa else w_qkv_b)[j]
        w_o = (w_o_a if swa else w_o_b)[j].astype(BF16)
        wqx = _expand_q_weight(w_qkv[:, :D_MODEL], ATTN_SCALE if swa else -ATTN_SCALE)
        wkv = w_qkv[:, D_MODEL:].astype(BF16)
        wk_t, wv_t = wkv[:, :D_KV].T, wkv[:, D_KV:].T
        qp, kp, vp, *blocks = _qkv_prompt_call(xp, norm_mix[i], wqx, wk_t, wv_t, rope_p if swa else None,
                                               batch, seq, not swa, f"qkv_prompt_{i}")
        qs, ks, vs = _qkv_sample_call(xs, norm_mix[i], wqx, wkv, rope_s if swa else None, f"qkv_sample_{i}")
        if swa:
            op = _swa_prompt_call(sinks_a[j], qp, kp, vp)
            os_, kw, vw = _swa_sample_call(sinks_a[j], qs, ks, vs, swa_kc, swa_vc, j)
            swa_kp.append(_positions_first(kp[..., seq - WINDOW:]))
            swa_vp.append(_positions_first(vp[..., seq - WINDOW:]))
            swa_ks.append(_positions_first(kw))
            swa_vs.append(_positions_first(vw))
        else:
            op = _sb_prompt_call(-bias_b[j], qp, blocks[0], blocks[1], u)
            os_ = _sb_sample_call(page_table, -bias_b[j], qs, ks, vs, u, sb_kc, sb_vc, j)
            sb_kp.append(_positions_first(kp))
            sb_vp.append(_positions_first(vp))
            sb_ks.append(ks.reshape(dec_batch, dec_seq, N_KV_HEADS, HEAD_DIM))
            sb_vs.append(vs.reshape(dec_batch, dec_seq, N_KV_HEADS, HEAD_DIM))
        g_final = norm_final if i == depth - 1 else None
        ffn_w = (w_o, norm_ffn[i], w_in_ffn[i].astype(BF16), conv_w[i], conv_b[i], w_down[i].astype(BF16))
        xp, cp = _ffn_call(xp, op, *ffn_w, None, g_final, seq, f"ffn_prompt_{i}")
        xs, cs = _ffn_call(xs, os_, *ffn_w, state_conv[i], g_final, None, f"ffn_sample_{i}")
        conv_p.append(cp[:, SUBLANES - (CONV_W - 1):])
        conv_s.append(cs)
    return (xp.reshape(batch, seq, D_MODEL), xs.reshape(dec_batch, dec_seq, D_MODEL),
            jnp.stack(swa_kp), jnp.stack(swa_vp), jnp.stack(sb_kp), jnp.stack(sb_vp), jnp.stack(conv_p),
            jnp.stack(swa_ks), jnp.stack(swa_vs), jnp.stack(sb_ks), jnp.stack(sb_vs), jnp.stack(conv_s))
```

```python
import functools

import jax
import jax.numpy as jnp
from jax import lax
from jax.experimental import pallas as pl
from jax.experimental.pallas import tpu as pltpu

F32 = jnp.float32
BF16 = jnp.bfloat16

D_MODEL = 1024
HEAD_DIM = 64
N_HEADS = 16
N_KV_HEADS = 4
GROUP = N_HEADS // N_KV_HEADS
D_KV = N_KV_HEADS * HEAD_DIM
DEC_SEQ = 8
WINDOW = 128
PAGE_SIZE = 128
ROPE_THETA = 10000.0
D_FF = 2816
CONV_W = 3
RMS_EPS = 1e-6
ATTN_SCALE = HEAD_DIM ** -0.5

LANES = 128
SUBLANES = 8
D_QX = N_HEADS * LANES
FF_CHUNK = 256
N_FF_CHUNKS = D_FF // FF_CHUNK
TOKEN_TILE = 512
FFN_TILE = 512
SB_BLOCK = 2 * LANES
PAGES_PER_STEP = 32
PAGE_BUFFERS = 3
VMEM_LIMIT = 56 * 1024 * 1024

assert D_FF % FF_CHUNK == 0 and TOKEN_TILE % SB_BLOCK == 0
assert DEC_SEQ == SUBLANES and WINDOW == LANES and PAGE_SIZE == LANES and 2 * HEAD_DIM == LANES


def _params(n_axes):
    return pltpu.CompilerParams(dimension_semantics=("arbitrary",) * n_axes,
                                vmem_limit_bytes=VMEM_LIMIT)


def _resident(shape, index_map):
    return pl.BlockSpec(shape, index_map, pipeline_mode=pl.Buffered(1))


def _rmsnorm(x, g):
    r = lax.rsqrt(jnp.mean(x * x, axis=-1, keepdims=True) + RMS_EPS)
    return x * r * g


def _dot(a, b):
    return jnp.dot(a, b, preferred_element_type=F32)


def _dot_nt(a, b):
    return lax.dot_general(a, b, (((1,), (1,)), ((), ())), preferred_element_type=F32)


def _kv_pair(kv):
    return kv // 2, kv % 2


def _rope_tile(x, cosf, sinf):
    lane = lax.broadcasted_iota(jnp.int32, x.shape, 1)
    first = (lane % HEAD_DIM) < (HEAD_DIM // 2)
    other = jnp.where(first, pltpu.roll(x, LANES - HEAD_DIM // 2, axis=1),
                      pltpu.roll(x, HEAD_DIM // 2, axis=1))
    return x * cosf + other * sinf


def _rope_rows(x, cos_t, sin_t):
    half = HEAD_DIM // 2
    out = []
    for kv in range(N_KV_HEADS):
        x1 = x[kv * HEAD_DIM:kv * HEAD_DIM + half]
        x2 = x[kv * HEAD_DIM + half:(kv + 1) * HEAD_DIM]
        out += [x1 * cos_t - x2 * sin_t, x2 * cos_t + x1 * sin_t]
    return jnp.concatenate(out, axis=0)


def _qkv_prompt_body(*refs, rope, blocked):
    refs = list(refs)
    x_ref, g_ref, wq_ref, wk_ref, wv_ref = refs[:5]
    refs = refs[5:]
    if rope:
        cos_ref, sin_ref, cos_t_ref, sin_t_ref = refs[:4]
        refs = refs[4:]
    q_ref, k_ref, v_ref = refs[:3]
    h = _rmsnorm(x_ref[...], g_ref[...]).astype(BF16)
    tm = h.shape[0]
    q = _dot(h, wq_ref[...])
    k_t = _dot_nt(wk_ref[...], h)
    v_t = _dot_nt(wv_ref[...], h)
    if rope:
        cosf, sinf = cos_ref[...], sin_ref[...]
        for j in range(N_HEADS):
            sl = slice(j * LANES, (j + 1) * LANES)
            q_ref[:, sl] = _rope_tile(q[:, sl], cosf, sinf).astype(q_ref.dtype)
        k_t = _rope_rows(k_t, cos_t_ref[...], sin_t_ref[...])
    else:
        q_ref[...] = q.astype(q_ref.dtype)
    k_ref[...] = k_t.reshape(N_KV_HEADS, HEAD_DIM, tm)
    v_ref[...] = v_t.reshape(N_KV_HEADS, HEAD_DIM, tm)
    if blocked:
        kb_ref, vb_ref = refs[3:5]
        for c in range(tm // SB_BLOCK):
            sl = slice(c * SB_BLOCK, (c + 1) * SB_BLOCK)
            kb_ref[c] = k_t[:, sl].reshape(N_KV_HEADS, HEAD_DIM, SB_BLOCK)
            vb_ref[c] = v_t[:, sl].reshape(N_KV_HEADS, HEAD_DIM, SB_BLOCK)


def _qkv_prompt_call(x, g, wqx, wk_t, wv_t, tables, batch, seq, blocked, name):
    tm = TOKEN_TILE
    tps = seq // tm
    rope = tables is not None
    row = lambda i: (i, 0)
    const = lambda i: (0, 0)
    in_specs = [pl.BlockSpec((tm, D_MODEL), row),
                pl.BlockSpec((1, D_MODEL), const),
                _resident((D_MODEL, D_QX), const),
                _resident((D_KV, D_MODEL), const),
                _resident((D_KV, D_MODEL), const)]
    args = [x, g.reshape(1, D_MODEL), wqx, wk_t, wv_t]
    if rope:
        cosf, sinf, cos_t, sin_t = tables
        in_specs += [pl.BlockSpec((tm, LANES), lambda i: (i % tps, 0))] * 2
        in_specs += [pl.BlockSpec((HEAD_DIM // 2, tm), lambda i: (0, i % tps))] * 2
        args += [cosf, sinf, cos_t, sin_t]
    kv_spec = pl.BlockSpec((None, N_KV_HEADS, HEAD_DIM, tm), lambda i: (i // tps, 0, 0, i % tps))
    kv_shape = jax.ShapeDtypeStruct((batch, N_KV_HEADS, HEAD_DIM, seq), F32)
    out_specs = [pl.BlockSpec((tm, D_QX), row), kv_spec, kv_spec]
    out_shape = [jax.ShapeDtypeStruct((batch * seq, D_QX), BF16), kv_shape, kv_shape]
    if blocked:
        nb = tm // SB_BLOCK
        b_spec = pl.BlockSpec((None, nb, N_KV_HEADS, HEAD_DIM, SB_BLOCK), lambda i: (i // tps, i % tps, 0, 0, 0))
        b_shape = jax.ShapeDtypeStruct((batch, seq // SB_BLOCK, N_KV_HEADS, HEAD_DIM, SB_BLOCK), F32)
        out_specs += [b_spec, b_spec]
        out_shape += [b_shape, b_shape]
    return pl.pallas_call(
        functools.partial(_qkv_prompt_body, rope=rope, blocked=blocked),
        grid=(batch * seq // tm,),
        in_specs=in_specs, out_specs=out_specs, out_shape=out_shape,
        compiler_params=_params(1), name=name,
    )(*args)


def _qkv_sample_body(*refs, rope):
    if rope:
        x_ref, g_ref, wq_ref, wkv_ref, cos_ref, sin_ref, q_ref, k_ref, v_ref = refs
    else:
        x_ref, g_ref, wq_ref, wkv_ref, q_ref, k_ref, v_ref = refs
    h = _rmsnorm(x_ref[...], g_ref[...]).astype(BF16)
    q = _dot(h, wq_ref[...])
    kv = _dot(h, wkv_ref[...])
    if rope:
        cosf, sinf = cos_ref[...], sin_ref[...]
        for j in range(N_HEADS):
            sl = slice(j * LANES, (j + 1) * LANES)
            q_ref[:, sl] = _rope_tile(q[:, sl], cosf, sinf)
        for j in range(D_KV // LANES):
            sl = slice(j * LANES, (j + 1) * LANES)
            k_ref[:, sl] = _rope_tile(kv[:, sl], cosf, sinf)
    else:
        q_ref[...] = q
        k_ref[...] = kv[:, :D_KV]
    v_ref[...] = kv[:, D_KV:]


def _qkv_sample_call(x, g, wqx, wkv, tables, name):
    t = x.shape[0]
    tm = min(TOKEN_TILE, t)
    rope = tables is not None
    row = lambda i: (i, 0)
    const = lambda i: (0, 0)
    in_specs = [pl.BlockSpec((tm, D_MODEL), row),
                pl.BlockSpec((1, D_MODEL), const),
                _resident((D_MODEL, D_QX), const),
                _resident((D_MODEL, 2 * D_KV), const)]
    args = [x, g.reshape(1, D_MODEL), wqx, wkv]
    if rope:
        in_specs += [pl.BlockSpec((tm, LANES), const)] * 2
        args += list(tables)
    return pl.pallas_call(
        functools.partial(_qkv_sample_body, rope=rope),
        grid=(t // tm,),
        in_specs=in_specs,
        out_specs=[pl.BlockSpec((tm, D_QX), row), pl.BlockSpec((tm, D_KV), row), pl.BlockSpec((tm, D_KV), row)],
        out_shape=[jax.ShapeDtypeStruct((t, D_QX), F32),
                   jax.ShapeDtypeStruct((t, D_KV), F32),
                   jax.ShapeDtypeStruct((t, D_KV), F32)],
        compiler_params=_params(1), name=name,
    )(*args)


def _rope_tables(pos):
    inv_freq = ROPE_THETA ** (-jnp.arange(0, HEAD_DIM, 2, dtype=F32) / HEAD_DIM)
    ang = pos.astype(F32)[:, None] * inv_freq[None, :]
    cos, sin = jnp.cos(ang), jnp.sin(ang)
    cosf = jnp.concatenate([cos, cos, cos, cos], axis=1)
    sinf = jnp.concatenate([-sin, sin, -sin, sin], axis=1)
    return cosf, sinf, cos.T, sin.T


def _expand_q_weight(w_q, scale):
    w = (w_q * scale).reshape(D_MODEL, N_HEADS, 1, HEAD_DIM)
    half = ((jnp.arange(N_HEADS) // GROUP) % 2)[None, :, None, None]
    w = jnp.where(half == jnp.arange(2)[None, None, :, None], w, 0.0)
    return w.reshape(D_MODEL, D_QX).astype(BF16)


def _half_mask(shape, half):
    lane = lax.broadcasted_iota(jnp.int32, shape, 1)
    return (lane >= HEAD_DIM) if half else (lane < HEAD_DIM)


def _merge_heads(even, odd, even_half, odd_half):
    if even_half != 0:
        even = pltpu.roll(even, HEAD_DIM, axis=1)
    if odd_half != 1:
        odd = pltpu.roll(odd, HEAD_DIM, axis=1)
    return jnp.where(_half_mask(even.shape, 0), even, odd)


def _merge_all_heads(parts):
    tiles = [_merge_heads(parts[2 * p][0], parts[2 * p + 1][0], parts[2 * p][1], parts[2 * p + 1][1])
             for p in range(N_HEADS // 2)]
    return jnp.concatenate(tiles, axis=1)


def _block_diag_q(qx):
    zeros = jnp.zeros((SUBLANES, LANES), qx.dtype)
    rows = []
    for h in range(N_HEADS):
        pair, _ = _kv_pair(h // GROUP)
        tile = qx[:, h * LANES:(h + 1) * LANES]
        rows.append(jnp.concatenate([tile, zeros] if pair == 0 else [zeros, tile], axis=1))
    return jnp.concatenate(rows, axis=0)


def _unpack_block_diag(acc):
    parts = []
    for h in range(N_HEADS):
        pair, half = _kv_pair(h // GROUP)
        parts.append((acc[h * SUBLANES:(h + 1) * SUBLANES, pair * LANES:(pair + 1) * LANES], half))
    return _merge_all_heads(parts)


def _per_row_heads(scalar_ref, rows_per_head, width):
    return jnp.concatenate([jnp.full((rows_per_head, width), scalar_ref[h], F32) for h in range(N_HEADS)], axis=0)


def _pad_rows(pad_scr, x):
    pad_scr[...] = jnp.zeros(pad_scr.shape, F32)
    pad_scr[0:DEC_SEQ, :] = x
    return pad_scr[...]


def _swa_prompt_body(sink_ref, q_ref, kp_ref, kc_ref, vp_ref, vc_ref, o_ref):
    j = pl.program_id(1)
    rows = GROUP * WINDOW
    a = lax.broadcasted_iota(jnp.int32, (rows, 2 * WINDOW), 0) % WINDOW
    c = lax.broadcasted_iota(jnp.int32, (rows, 2 * WINDOW), 1)
    mask = (c >= a) & (c <= a + WINDOW) & ((j > 0) | (c >= WINDOW))
    kvs = range(N_KV_HEADS)

    def band(prev, cur, kv):
        rows2 = slice(2 * (kv // 2), 2 * (kv // 2) + 2)
        return jnp.concatenate([prev[rows2].reshape(LANES, WINDOW), cur[rows2].reshape(LANES, WINDOW)],
                               axis=1).astype(BF16)

    zs = []
    for kv in kvs:
        q4 = jnp.concatenate([q_ref[:, h * LANES:(h + 1) * LANES] for h in range(kv * GROUP, (kv + 1) * GROUP)], axis=0)
        zs.append(jnp.where(mask, _dot(q4, band(kp_ref, kc_ref, kv)), -jnp.inf))
    ps, denoms = [], []
    for kv in kvs:
        sink = jnp.concatenate([jnp.full((WINDOW, LANES), sink_ref[h], F32)
                                for h in range(kv * GROUP, (kv + 1) * GROUP)], axis=0)
        m = jnp.maximum(jnp.max(zs[kv], axis=-1, keepdims=True), sink)
        p = jnp.exp(zs[kv] - jnp.concatenate([m, m], axis=1)).astype(BF16)
        denoms.append(_dot(p, jnp.ones((2 * WINDOW, LANES), BF16)) + jnp.exp(sink - m))
        ps.append(p)
    parts = []
    for kv in kvs:
        o4 = _dot_nt(ps[kv], band(vp_ref, vc_ref, kv)) / denoms[kv]
        parts += [(o4[g * WINDOW:(g + 1) * WINDOW], _kv_pair(kv)[1]) for g in range(GROUP)]
    o_ref[...] = _merge_all_heads(parts).astype(o_ref.dtype)


def _swa_prompt_call(sink, qx, k_t, v_t):
    batch, _, _, seq = k_t.shape
    nb = seq // WINDOW
    kv_spec = lambda f: pl.BlockSpec((None, N_KV_HEADS, HEAD_DIM, WINDOW), f)
    cur = lambda b, j: (b, 0, 0, j)
    prev = lambda b, j: (b, 0, 0, jnp.maximum(j - 1, 0))
    return pl.pallas_call(
        _swa_prompt_body,
        grid=(batch, nb),
        in_specs=[pl.BlockSpec(memory_space=pltpu.SMEM),
                  pl.BlockSpec((WINDOW, D_QX), lambda b, j: (b * nb + j, 0)),
                  kv_spec(prev), kv_spec(cur), kv_spec(prev), kv_spec(cur)],
        out_specs=pl.BlockSpec((WINDOW, D_MODEL), lambda b, j: (b * nb + j, 0)),
        out_shape=jax.ShapeDtypeStruct((batch * seq, D_MODEL), BF16),
        compiler_params=_params(2), name="swa_prompt",
    )(sink, qx, k_t, k_t, v_t, v_t)


def _swa_sample_body(sink_ref, q_ref, kn_ref, vn_ref, kc_ref, vc_ref, o_ref, kw_ref, vw_ref, pad_scr):
    rows = N_HEADS * DEC_SEQ
    t = lax.broadcasted_iota(jnp.int32, (rows, WINDOW), 0) % DEC_SEQ
    c = lax.broadcasted_iota(jnp.int32, (rows, WINDOW), 1)
    kc_t = kc_ref[...].reshape(D_KV, WINDOW)
    vc_t = vc_ref[...].reshape(D_KV, WINDOW)
    kn = _pad_rows(pad_scr, kn_ref[...])
    vn = _pad_rows(pad_scr, vn_ref[...])

    lane = lax.broadcasted_iota(jnp.int32, (D_KV, WINDOW), 1)
    keep = WINDOW - DEC_SEQ
    shift = lambda x: pltpu.roll(x, keep, axis=1)
    kw_ref[...] = jnp.where(lane < keep, shift(kc_t), shift(kn.T)).reshape(N_KV_HEADS, HEAD_DIM, WINDOW)
    vw_ref[...] = jnp.where(lane < keep, shift(vc_t), shift(vn.T)).reshape(N_KV_HEADS, HEAD_DIM, WINDOW)

    qbd = _block_diag_q(q_ref[...]).astype(BF16)
    sink = _per_row_heads(sink_ref, DEC_SEQ, LANES)
    zc = jnp.where(c >= t, _dot(qbd, kc_t.astype(BF16)), -jnp.inf)
    zn = jnp.where(c <= t, _dot_nt(qbd, kn.astype(BF16)), -jnp.inf)
    m = jnp.maximum(jnp.maximum(jnp.max(zc, axis=-1, keepdims=True), jnp.max(zn, axis=-1, keepdims=True)), sink)
    pc = jnp.exp(zc - m).astype(BF16)
    pn = jnp.exp(zn - m).astype(BF16)
    ones = jnp.ones((WINDOW, LANES), BF16)
    denom = _dot(pc, ones) + _dot(pn, ones) + jnp.exp(sink - m)
    acc = _dot_nt(pc, vc_t.astype(BF16)) + _dot(pn, vn.astype(BF16))
    o_ref[...] = _unpack_block_diag(acc / jnp.concatenate([denom, denom], axis=1))


def _swa_sample_call(sink, qx, k_new, v_new, k_cache_t, v_cache_t, layer):
    n = k_cache_t.shape[1]
    seq_rows = lambda s: (s, 0)
    cache_in = pl.BlockSpec((None, None, N_KV_HEADS, HEAD_DIM, WINDOW), lambda s: (layer, s, 0, 0, 0))
    cache_out = pl.BlockSpec((None, N_KV_HEADS, HEAD_DIM, WINDOW), lambda s: (s, 0, 0, 0))
    win_shape = jax.ShapeDtypeStruct((n, N_KV_HEADS, HEAD_DIM, WINDOW), F32)
    return pl.pallas_call(
        _swa_sample_body,
        grid=(n,),
        in_specs=[pl.BlockSpec(memory_space=pltpu.SMEM),
                  pl.BlockSpec((DEC_SEQ, D_QX), seq_rows),
                  pl.BlockSpec((DEC_SEQ, D_KV), seq_rows),
                  pl.BlockSpec((DEC_SEQ, D_KV), seq_rows),
                  cache_in, cache_in],
        out_specs=[pl.BlockSpec((DEC_SEQ, D_MODEL), seq_rows), cache_out, cache_out],
        out_shape=[jax.ShapeDtypeStruct((n * DEC_SEQ, D_MODEL), F32), win_shape, win_shape],
        scratch_shapes=[pltpu.VMEM((WINDOW, D_KV), F32)],
        compiler_params=_params(1), name="swa_sample",
    )(sink, qx, k_new, v_new, k_cache_t, v_cache_t)


def _suffix_matrix():
    j = jnp.arange(2 * LANES)[:, None] % LANES
    s = jnp.arange(2 * LANES)[None, :]
    return ((s >= LANES) | (j >= s)).astype(BF16)


def _hi_lo(x):
    hi = x.astype(BF16)
    lo = (x - hi.astype(F32)).astype(BF16)
    return jnp.concatenate([hi, lo], axis=1)


def _sb_log_rest(neg_z_blocks, masks):
    rows = []
    for zn, mk in zip(neg_z_blocks, masks):
        lr = jnp.minimum(zn, 0.0) - jnp.log(1.0 + jnp.exp(-jnp.abs(zn)))
        rows.append(_hi_lo(lr if mk is None else jnp.where(mk, lr, 0.0)))
    return jnp.concatenate(rows, axis=0)


def _sb_weights(neg_z_blocks, masks, sums, carry):
    m = neg_z_blocks[0].shape[0]
    weights = []
    for n, (zn, mk) in enumerate(zip(neg_z_blocks, masks)):
        s = sums[n * m:(n + 1) * m]
        a = jnp.exp(s[:, :LANES] + carry - zn)
        weights.append(a if mk is None else jnp.where(mk, a, 0.0))
        carry = carry + s[:, LANES:]
    return weights, carry


def _sb_blocks(neg_z_blocks, masks, u, carry):
    sums = _dot(_sb_log_rest(neg_z_blocks, masks), u)
    return _sb_weights(neg_z_blocks, masks, sums, carry)


def _sb_prompt_body(nbias_ref, q_ref, k_ref, v_ref, u_ref, o_ref, carry_scr, acc_scr):
    i = pl.program_id(1)
    rows = GROUP * SB_BLOCK
    sub = SB_BLOCK // LANES
    u = u_ref[...]
    t = lax.broadcasted_iota(jnp.int32, (rows, LANES), 0) % SB_BLOCK
    s = lax.broadcasted_iota(jnp.int32, (rows, LANES), 1)
    diag_masks = [(s + n * LANES) < t for n in reversed(range(sub))]

    def block(c, masks, first):
        kvs = range(N_KV_HEADS)
        tile = lambda ref, kv: ref[c, 2 * (kv // 2):2 * (kv // 2) + 2].reshape(LANES, SB_BLOCK).astype(BF16)
        zs = []
        for kv in kvs:
            heads = range(kv * GROUP, (kv + 1) * GROUP)
            q4 = jnp.concatenate([q_ref[:, h * LANES:(h + 1) * LANES] for h in heads], axis=0)
            nbias = jnp.concatenate([jnp.full((SB_BLOCK, LANES), nbias_ref[h], F32) for h in heads], axis=0)
            zn = _dot(q4, tile(k_ref, kv))
            zs.append([zn[:, n * LANES:(n + 1) * LANES] + nbias for n in reversed(range(sub))])
        log_rest = [_sb_log_rest(zs[kv], masks) for kv in kvs]
        sums = [_dot(log_rest[kv], u) for kv in kvs]
        for kv in kvs:
            carry = jnp.zeros((rows, LANES), F32) if first else carry_scr[kv]
            weights, carry = _sb_weights(zs[kv], masks, sums[kv], carry)
            a = jnp.concatenate(list(reversed(weights)), axis=1).astype(BF16)
            pv = _dot_nt(a, tile(v_ref, kv))
            acc_scr[kv] = pv if first else acc_scr[kv] + pv
            carry_scr[kv] = carry

    block(i, diag_masks, True)
    lax.fori_loop(0, i, lambda n, _: (block(i - 1 - n, [None] * sub, False), 0)[1], 0)
    parts = []
    for kv in range(N_KV_HEADS):
        acc = acc_scr[kv]
        parts += [(acc[g * SB_BLOCK:(g + 1) * SB_BLOCK], _kv_pair(kv)[1]) for g in range(GROUP)]
    o_ref[...] = _merge_all_heads(parts).astype(o_ref.dtype)


def _sb_prompt_call(neg_bias, neg_qx, k_blocks, v_blocks, u):
    batch, nb = k_blocks.shape[:2]
    rows = GROUP * SB_BLOCK
    kv_spec = pl.BlockSpec((None, nb, N_KV_HEADS, HEAD_DIM, SB_BLOCK), lambda b, i: (b, 0, 0, 0, 0))
    return pl.pallas_call(
        _sb_prompt_body,
        grid=(batch, nb),
        in_specs=[pl.BlockSpec(memory_space=pltpu.SMEM),
                  pl.BlockSpec((SB_BLOCK, D_QX), lambda b, i: (b * nb + i, 0)),
                  kv_spec, kv_spec,
                  pl.BlockSpec((2 * LANES, 2 * LANES), lambda b, i: (0, 0))],
        out_specs=pl.BlockSpec((SB_BLOCK, D_MODEL), lambda b, i: (b * nb + i, 0)),
        out_shape=jax.ShapeDtypeStruct((batch * nb * SB_BLOCK, D_MODEL), BF16),
        scratch_shapes=[pltpu.VMEM((N_KV_HEADS, rows, LANES), F32),
                        pltpu.VMEM((N_KV_HEADS, rows, LANES), F32)],
        compiler_params=_params(2), name="sb_prompt",
    )(neg_bias, neg_qx, k_blocks, v_blocks, u)


def _page_copies(pt_ref, pool_k, pool_v, kbuf, vbuf, sems, layer, n_chunks, step, slot):
    n_pages = pt_ref.shape[1]
    seq, chunk = step // n_chunks, step % n_chunks
    copies = []
    for j in range(PAGES_PER_STEP):
        page = pt_ref[seq, n_pages - 1 - (chunk * PAGES_PER_STEP + j)]
        copies.append(pltpu.make_async_copy(pool_k.at[layer, page], kbuf.at[slot, j], sems.at[slot, 0]))
        copies.append(pltpu.make_async_copy(pool_v.at[layer, page], vbuf.at[slot, j], sems.at[slot, 1]))
    return copies


def _sb_sample_body(pt_ref, nbias_ref, q_ref, kn_ref, vn_ref, u_ref, pool_k, pool_v, o_ref,
                    qbd_scr, bias_scr, carry_scr, acc_scr, pad_scr, kbuf, vbuf, sems, *, layer, n_steps, n_chunks):
    c = pl.program_id(1)
    step = pl.program_id(0) * n_chunks + c
    slot = step % PAGE_BUFFERS
    rows = N_HEADS * DEC_SEQ
    u = u_ref[...]
    copies = functools.partial(_page_copies, pt_ref, pool_k, pool_v, kbuf, vbuf, sems, layer, n_chunks)

    @pl.when(step == 0)
    def _first_fetches():
        for d in range(PAGE_BUFFERS - 1):
            for cp in copies(d, d):
                cp.start()

    @pl.when(c == 0)
    def _new_tokens():
        qbd = _block_diag_q(q_ref[...]).astype(BF16)
        nbias = _per_row_heads(nbias_ref, DEC_SEQ, LANES)
        qbd_scr[...] = qbd
        bias_scr[...] = nbias
        kn = _pad_rows(pad_scr, kn_ref[...]).astype(BF16)
        vn = _pad_rows(pad_scr, vn_ref[...]).astype(BF16)
        t = lax.broadcasted_iota(jnp.int32, (rows, LANES), 0) % DEC_SEQ
        s = lax.broadcasted_iota(jnp.int32, (rows, LANES), 1)
        (a,), carry = _sb_blocks([_dot_nt(qbd, kn) + nbias], [s < t], u, jnp.zeros((rows, LANES), F32))
        carry_scr[...] = carry
        acc_scr[...] = _dot(a.astype(BF16), vn)

    ahead = step + (PAGE_BUFFERS - 1)
    for cp in copies(step, slot):
        cp.wait()
    for cp in copies(jnp.minimum(ahead, n_steps - 1), ahead % PAGE_BUFFERS):
        cp.start()

    qbd, nbias = qbd_scr[...], bias_scr[...]
    page = lambda buf, j: buf[slot, j].reshape(D_KV, PAGE_SIZE).astype(BF16)
    k_t = jnp.concatenate([page(kbuf, j) for j in range(PAGES_PER_STEP)], axis=1)
    v_t = jnp.concatenate([page(vbuf, j) for j in range(PAGES_PER_STEP)], axis=1)
    zn = _dot(qbd, k_t)
    zs = [zn[:, n * LANES:(n + 1) * LANES] + nbias for n in range(PAGES_PER_STEP)]
    weights, carry = _sb_blocks(zs, [None] * PAGES_PER_STEP, u, carry_scr[...])
    acc = acc_scr[...] + _dot_nt(jnp.concatenate(weights, axis=1).astype(BF16), v_t)
    carry_scr[...] = carry
    acc_scr[...] = acc

    @pl.when(c == n_chunks - 1)
    def _sequence_done():
        o_ref[...] = _unpack_block_diag(acc)

    @pl.when(step == n_steps - 1)
    def _drain():
        for d in range(1, PAGE_BUFFERS):
            for cp in copies(step, (step + d) % PAGE_BUFFERS):
                cp.wait()


def _sb_sample_call(page_table, neg_bias, neg_qx, k_new, v_new, u, pool_k_t, pool_v_t, layer):
    n, n_pages = page_table.shape
    assert n_pages % PAGES_PER_STEP == 0
    seq_rows = lambda s, c, pt: (s, 0)
    n_chunks = n_pages // PAGES_PER_STEP
    assert n * n_chunks >= PAGE_BUFFERS
    page_buf = pltpu.VMEM((PAGE_BUFFERS, PAGES_PER_STEP, N_KV_HEADS, HEAD_DIM, PAGE_SIZE), F32)
    rows = N_HEADS * DEC_SEQ
    grid_spec = pltpu.PrefetchScalarGridSpec(
        num_scalar_prefetch=1,
        grid=(n, n_chunks),
        in_specs=[pl.BlockSpec(memory_space=pltpu.SMEM),
                  pl.BlockSpec((DEC_SEQ, D_QX), seq_rows),
                  pl.BlockSpec((DEC_SEQ, D_KV), seq_rows),
                  pl.BlockSpec((DEC_SEQ, D_KV), seq_rows),
                  pl.BlockSpec((2 * LANES, 2 * LANES), lambda s, c, pt: (0, 0)),
                  pl.BlockSpec(memory_space=pl.ANY),
                  pl.BlockSpec(memory_space=pl.ANY)],
        out_specs=pl.BlockSpec((DEC_SEQ, D_MODEL), seq_rows),
        scratch_shapes=[pltpu.VMEM((rows, D_KV), BF16),
                        pltpu.VMEM((rows, LANES), F32),
                        pltpu.VMEM((rows, LANES), F32),
                        pltpu.VMEM((rows, D_KV), F32),
                        pltpu.VMEM((PAGE_SIZE, D_KV), F32),
                        page_buf, page_buf,
                        pltpu.SemaphoreType.DMA((PAGE_BUFFERS, 2))],
    )
    return pl.pallas_call(
        functools.partial(_sb_sample_body, layer=layer, n_steps=n * n_chunks, n_chunks=n_chunks),
        grid_spec=grid_spec,
        out_shape=jax.ShapeDtypeStruct((n * DEC_SEQ, D_MODEL), F32),
        compiler_params=_params(2), name="sb_sample",
    )(page_table, neg_bias, neg_qx, k_new, v_new, u, pool_k_t, pool_v_t)


def _ffn_body(*refs, sample, tiles_per_seq, final):
    refs = list(refs)
    x_ref, o_ref, wo_ref, g_ref, win_ref, cw_ref, cb_ref, wd_ref = refs[:8]
    refs = refs[8:]
    prev_ref = refs.pop(0) if sample else None
    gf_ref = refs.pop(0) if final else None
    xo_ref, conv_ref, h_scr, acc_scr = refs[:4]
    carry_scr = None if sample else refs[4]
    i = pl.program_id(0)
    tm = x_ref.shape[0]
    x1 = x_ref[...] + _dot(o_ref[...].astype(BF16), wo_ref[...])
    h_scr[...] = _rmsnorm(x1, g_ref[...]).astype(BF16)
    acc_scr[...] = x1
    if not sample:
        @pl.when(i % tiles_per_seq == 0)
        def _sequence_start():
            carry_scr[...] = jnp.zeros(carry_scr.shape, F32)

    def activation(c):
        sl = slice(c * FF_CHUNK, (c + 1) * FF_CHUNK)
        h = h_scr[...]
        gate = _dot(h, win_ref[:, sl])
        up = _dot(h, win_ref[:, D_FF + c * FF_CHUNK:D_FF + (c + 1) * FF_CHUNK])
        cw = cw_ref[:, sl]
        if sample:
            nseq = tm // DEC_SEQ
            prev = prev_ref[:, :, sl]
            expand = lambda p: jnp.broadcast_to(p, (nseq, DEC_SEQ, FF_CHUNK)).reshape(tm, FF_CHUNK)
            p0, p1 = expand(prev[:, 0:1, :]), expand(prev[:, 1:2, :])
            t = lax.broadcasted_iota(jnp.int32, (tm, FF_CHUNK), 0) % DEC_SEQ
            g1 = jnp.where(t == 0, p1, pltpu.roll(gate, 1, axis=0))
            g2 = jnp.where(t == 0, p0, jnp.where(t == 1, p1, pltpu.roll(gate, 2, axis=0)))
            conv_ref[:, :, sl] = gate.reshape(nseq, DEC_SEQ, FF_CHUNK)[:, DEC_SEQ - (CONV_W - 1):, :]
        else:
            prev8 = carry_scr[:, sl]
            top = gate[0:SUBLANES]
            r8 = lax.broadcasted_iota(jnp.int32, (SUBLANES, FF_CHUNK), 0)
            top1 = jnp.where(r8 < 1, pltpu.roll(prev8, 1, axis=0), pltpu.roll(top, 1, axis=0))
            top2 = jnp.where(r8 < 2, pltpu.roll(prev8, 2, axis=0), pltpu.roll(top, 2, axis=0))
            g1 = jnp.concatenate([top1, pltpu.roll(gate, 1, axis=0)[SUBLANES:]], axis=0)
            g2 = jnp.concatenate([top2, pltpu.roll(gate, 2, axis=0)[SUBLANES:]], axis=0)
            last8 = gate[tm - SUBLANES:]
            carry_scr[:, sl] = last8
            conv_ref[:, sl] = last8
        gc = cb_ref[:, sl] + cw[0:1] * g2 + cw[1:2] * g1 + cw[2:3] * gate
        return (gc / (1.0 + jnp.exp(-gc)) * up).astype(BF16)

    act = activation(0)
    for c in range(1, N_FF_CHUNKS):
        nxt = activation(c)
        acc_scr[...] += _dot(act, wd_ref[(c - 1) * FF_CHUNK:c * FF_CHUNK, :])
        act = nxt
    x2 = acc_scr[...] + _dot(act, wd_ref[(N_FF_CHUNKS - 1) * FF_CHUNK:, :])
    xo_ref[...] = _rmsnorm(x2, gf_ref[...]) if final else x2


def _ffn_call(x, o, wo, g, w_in, cw, cb, wd, prev, g_final, seq, name):
    t = x.shape[0]
    sample = prev is not None
    final = g_final is not None
    tm = min(FFN_TILE, t)
    tiles_per_seq = None if sample else seq // tm
    row = lambda i: (i, 0)
    const = lambda i: (0, 0)
    in_specs = [pl.BlockSpec((tm, D_MODEL), row),
                pl.BlockSpec((tm, D_MODEL), row),
                _resident((D_MODEL, D_MODEL), const),
                pl.BlockSpec((1, D_MODEL), const),
                _resident((D_MODEL, 2 * D_FF), const),
                pl.BlockSpec((CONV_W, D_FF), const),
                pl.BlockSpec((1, D_FF), const),
                _resident((D_FF, D_MODEL), const)]
    args = [x, o, wo, g.reshape(1, D_MODEL), w_in, cw, cb.reshape(1, D_FF), wd]
    scratch = [pltpu.VMEM((tm, D_MODEL), BF16), pltpu.VMEM((tm, D_MODEL), F32)]
    if sample:
        nseq = tm // DEC_SEQ
        state_spec = pl.BlockSpec((nseq, CONV_W - 1, D_FF), lambda i: (i, 0, 0))
        in_specs.append(state_spec)
        args.append(prev)
        conv_spec = state_spec
        conv_shape = jax.ShapeDtypeStruct((t // DEC_SEQ, CONV_W - 1, D_FF), F32)
    else:
        conv_spec = pl.BlockSpec((None, SUBLANES, D_FF), lambda i: (i // tiles_per_seq, 0, 0))
        conv_shape = jax.ShapeDtypeStruct((t // seq, SUBLANES, D_FF), F32)
        scratch.append(pltpu.VMEM((SUBLANES, D_FF), F32))
    if final:
        in_specs.append(pl.BlockSpec((1, D_MODEL), const))
        args.append(g_final.reshape(1, D_MODEL))
    return pl.pallas_call(
        functools.partial(_ffn_body, sample=sample, tiles_per_seq=tiles_per_seq, final=final),
        grid=(t // tm,),
        in_specs=in_specs,
        out_specs=[pl.BlockSpec((tm, D_MODEL), row), conv_spec],
        out_shape=[jax.ShapeDtypeStruct((t, D_MODEL), F32), conv_shape],
        scratch_shapes=scratch,
        compiler_params=_params(1), name=name,
    )(*args)


def _positions_last(x):
    return jnp.moveaxis(x, -3, -1)


def _positions_first(x):
    return jnp.moveaxis(x, -1, -3)


def kernel(x_prompt, x_sample, cache_swa_k, cache_swa_v, cache_sb_k, cache_sb_v, state_conv, page_table,
           norm_mix, norm_ffn, norm_final, w_qkv_a, sinks_a, w_o_a, w_qkv_b, bias_b, w_o_b,
           w_in_ffn, conv_w, conv_b, w_down):
    batch, seq, _ = x_prompt.shape
    dec_batch, dec_seq, _ = x_sample.shape
    depth = norm_mix.shape[0]
    past_len = page_table.shape[1] * cache_sb_k.shape[2]
    assert dec_seq == DEC_SEQ and cache_swa_k.shape[2] == WINDOW and cache_sb_k.shape[2] == PAGE_SIZE
    xp = x_prompt.reshape(batch * seq, D_MODEL)
    xs = x_sample.reshape(dec_batch * dec_seq, D_MODEL)
    rope_p = _rope_tables(jnp.arange(seq))
    rope_s = _rope_tables(past_len + jnp.arange(min(TOKEN_TILE, dec_batch * dec_seq)) % dec_seq)[:2]
    u = _suffix_matrix()
    swa_kc, swa_vc = _positions_last(cache_swa_k), _positions_last(cache_swa_v)
    sb_kc, sb_vc = _positions_last(cache_sb_k), _positions_last(cache_sb_v)

    swa_kp, swa_vp, swa_ks, swa_vs = [], [], [], []
    sb_kp, sb_vp, sb_ks, sb_vs = [], [], [], []
    conv_p, conv_s = [], []
    for i in range(depth):
        j = i // 2
        swa = i % 2 == 0
        w_qkv = (w_qkv_a if swa else w_qkv_b)[j]
        w_o = (w_o_a if swa else w_o_b)[j].astype(BF16)
        wqx = _expand_q_weight(w_qkv[:, :D_MODEL], ATTN_SCALE if swa else -ATTN_SCALE)
        wkv = w_qkv[:, D_MODEL:].astype(BF16)
        wk_t, wv_t = wkv[:, :D_KV].T, wkv[:, D_KV:].T
        qp, kp, vp, *blocks = _qkv_prompt_call(xp, norm_mix[i], wqx, wk_t, wv_t, rope_p if swa else None,
                                               batch, seq, not swa, f"qkv_prompt_{i}")
        qs, ks, vs = _qkv_sample_call(xs, norm_mix[i], wqx, wkv, rope_s if swa else None, f"qkv_sample_{i}")
        if swa:
            op = _swa_prompt_call(sinks_a[j], qp, kp, vp)
            os_, kw, vw = _swa_sample_call(sinks_a[j], qs, ks, vs, swa_kc, swa_vc, j)
            swa_kp.append(_positions_first(kp[..., seq - WINDOW:]))
            swa_vp.append(_positions_first(vp[..., seq - WINDOW:]))
            swa_ks.append(_positions_first(kw))
            swa_vs.append(_positions_first(vw))
        else:
            op = _sb_prompt_call(-bias_b[j], qp, blocks[0], blocks[1], u)
            os_ = _sb_sample_call(page_table, -bias_b[j], qs, ks, vs, u, sb_kc, sb_vc, j)
            sb_kp.append(_positions_first(kp))
            sb_vp.append(_positions_first(vp))
            sb_ks.append(ks.reshape(dec_batch, dec_seq, N_KV_HEADS, HEAD_DIM))
            sb_vs.append(vs.reshape(dec_batch, dec_seq, N_KV_HEADS, HEAD_DIM))
        g_final = norm_final if i == depth - 1 else None
        ffn_w = (w_o, norm_ffn[i], w_in_ffn[i].astype(BF16), conv_w[i], conv_b[i], w_down[i].astype(BF16))
        xp, cp = _ffn_call(xp, op, *ffn_w, None, g_final, seq, f"ffn_prompt_{i}")
        xs, cs = _ffn_call(xs, os_, *ffn_w, state_conv[i], g_final, None, f"ffn_sample_{i}")
        conv_p.append(cp[:, SUBLANES - (CONV_W - 1):])
        conv_s.append(cs)
    return (xp.reshape(batch, seq, D_MODEL), xs.reshape(dec_batch, dec_seq, D_MODEL),
            jnp.stack(swa_kp), jnp.stack(swa_vp), jnp.stack(sb_kp), jnp.stack(sb_vp), jnp.stack(conv_p),
            jnp.stack(swa_ks), jnp.stack(swa_vs), jnp.stack(sb_ks), jnp.stack(sb_vs), jnp.stack(conv_s))
```
